```python
import jax, jax.numpy as jnp
from jax import lax
import numpy as np

D_MODEL = 1024
BATCH = 4
SEQ = 4096
DEPTH = 4
DEC_BATCH = 128
DEC_SEQ = 8
PAST_LEN = 2048
PAGE_SIZE = 128

N_HEADS = 8
HEAD_DIM = 64
KV_HEADS = 4
IDX_HEADS = 4
IDX_DIM = 64
TOPK_MAX = 256
Q_BLOCK = 128
CONV_DIM = 512
CONV_W = 3
N_EXPERTS = 32
TOP_K = 4
D_EXPERT = 1024
SWIGLU_LIMIT = 7.0
SWIGLU_ALPHA = 1.702
EXP_BLOCK = 128
ROPE_THETA = 10000.0
LN_EPS = 1e-5
ATTN_WIDTH = N_HEADS * HEAD_DIM
KV_WIDTH = KV_HEADS * HEAD_DIM
DEEPNORM_ALPHA = (2 * DEPTH) ** 0.25
DEEPNORM_BETA = (8 * DEPTH) ** -0.25
IN_SIZES = (ATTN_WIDTH, KV_WIDTH, KV_WIDTH, IDX_HEADS * IDX_DIM, IDX_DIM, IDX_HEADS,
            CONV_DIM, CONV_DIM, CONV_DIM, D_MODEL, D_MODEL)
IN_COLS = sum(IN_SIZES)

kernel_name = 'hybrid_dsa_shortconv_moe_deepnorm_step'


def layer_norm(x, g, b):
    xf = x.astype(jnp.float32)
    mu = jnp.mean(xf, axis=-1, keepdims=True)
    var = jnp.mean(jnp.square(xf - mu), axis=-1, keepdims=True)
    y = (xf - mu) * lax.rsqrt(var + LN_EPS) * g.astype(jnp.float32) + b.astype(jnp.float32)
    return y.astype(x.dtype)


def rope(x, pos):
    half = x.shape[-1] // 2
    inv = ROPE_THETA ** (-jnp.arange(half, dtype=jnp.float32) / half)
    ang = pos.astype(jnp.float32)[:, None] * inv[None, :]
    cos = jnp.cos(ang)[:, None, :]
    sin = jnp.sin(ang)[:, None, :]
    xf = x.astype(jnp.float32)
    x1, x2 = xf[..., :half], xf[..., half:]
    return jnp.concatenate([x1 * cos - x2 * sin, x2 * cos + x1 * sin], axis=-1).astype(x.dtype)


def dsa_block(q, qi, wi, q_pos, k, v, ki, k_pos, topk):
    b, t = q.shape[:2]
    rel = jax.nn.relu(jnp.einsum('bthd,bsd->bths', qi.astype(jnp.float32), ki.astype(jnp.float32)) * IDX_DIM ** -0.5)
    score = jnp.einsum('bth,bths->bts', wi.astype(jnp.float32), rel)
    causal = k_pos[None, :] <= q_pos[:, None]
    score = jnp.where(causal[None], score, -jnp.inf)
    _, sel = lax.top_k(score, topk)
    gather = jax.vmap(lambda rows, idx: rows[idx])
    k_sel = gather(k, sel)
    v_sel = gather(v, sel)
    valid = k_pos[sel] <= q_pos[None, :, None]
    qg = q.reshape(b, t, KV_HEADS, N_HEADS // KV_HEADS, HEAD_DIM)
    logits = jnp.einsum('bthgd,btkhd->bthgk', qg.astype(jnp.float32), k_sel.astype(jnp.float32)) * HEAD_DIM ** -0.5
    logits = jnp.where(valid[:, :, None, None, :], logits, -jnp.inf)
    p = jax.nn.softmax(logits, axis=-1)
    o = jnp.einsum('bthgk,btkhd->bthgd', p.astype(v.dtype), v_sel)
    return o.reshape(b, t, ATTN_WIDTH)


def sparse_attention(q, qi, wi, q_pos, k, v, ki, k_pos, topk):
    b, t = q.shape[:2]
    if t <= Q_BLOCK or t % Q_BLOCK != 0:
        return dsa_block(q, qi, wi, q_pos, k, v, ki, k_pos, topk)
    nb = t // Q_BLOCK

    def blk(a):
        return jnp.moveaxis(a.reshape(a.shape[0], nb, Q_BLOCK, *a.shape[2:]), 1, 0)

    def one(args):
        qb, qib, wib, pb = args
        return dsa_block(qb, qib, wib, pb, k, v, ki, k_pos, topk)

    out = lax.map(one, (blk(q), blk(qi), blk(wi), q_pos.reshape(nb, Q_BLOCK)))
    return jnp.moveaxis(out, 0, 1).reshape(b, t, ATTN_WIDTH)


def swiglu_expert(x, w1, b1, w2, b2):
    h = jnp.dot(x, w1) + b1
    glu = jnp.minimum(h[..., 0::2], SWIGLU_LIMIT)
    lin = jnp.clip(h[..., 1::2], -SWIGLU_LIMIT, SWIGLU_LIMIT)
    act = glu * jax.nn.sigmoid(SWIGLU_ALPHA * glu) * (lin + 1.0)
    return jnp.dot(act, w2) + b2


def moe_ffn(x, w_r, b_r, w1, b1, w2, b2):
    shp = x.shape
    xt = x.reshape(-1, shp[-1])
    n = xt.shape[0]
    logits = jnp.dot(xt, w_r).astype(jnp.float32) + b_r.astype(jnp.float32)
    top_v, top_e = lax.top_k(logits, TOP_K)
    gate = jax.nn.softmax(top_v, axis=-1)
    flat_e = top_e.reshape(-1)
    flat_tok = jnp.repeat(jnp.arange(n, dtype=jnp.int32), TOP_K)
    flat_g = gate.reshape(-1)
    order = jnp.argsort(flat_e)
    e_sorted = flat_e[order]
    counts = jnp.zeros((N_EXPERTS,), jnp.int32).at[flat_e].add(1)
    padded = (counts + EXP_BLOCK - 1) // EXP_BLOCK * EXP_BLOCK
    pad_end = jnp.cumsum(padded)
    pad_start = pad_end - padded
    srt_start = jnp.cumsum(counts) - counts
    rank = jnp.arange(n * TOP_K, dtype=jnp.int32) - srt_start[e_sorted]
    dest = pad_start[e_sorted] + rank
    n_rows = ((n * TOP_K + EXP_BLOCK - 1) // EXP_BLOCK + N_EXPERTS) * EXP_BLOCK
    row_tok = jnp.zeros((n_rows,), jnp.int32).at[dest].set(flat_tok[order])
    row_g = jnp.zeros((n_rows,), jnp.float32).at[dest].set(flat_g[order])
    n_blk = n_rows // EXP_BLOCK
    blk_start = jnp.arange(n_blk, dtype=jnp.int32) * EXP_BLOCK
    blk_e = jnp.minimum(jnp.sum(blk_start[:, None] >= pad_end[None, :], axis=1), N_EXPERTS - 1)
    x_blk = xt[row_tok].reshape(n_blk, EXP_BLOCK, shp[-1])

    def run(args):
        xb, e = args
        return swiglu_expert(xb, w1[e], b1[e], w2[e], b2[e])

    y_rows = lax.map(run, (x_blk, blk_e)).reshape(n_rows, shp[-1])
    y = jnp.zeros_like(xt).at[row_tok].add(row_g[:, None].astype(y_rows.dtype) * y_rows)
    return y.reshape(shp)


def trunk_layer(x, past_len, k_past, v_past, ki_past, conv_hist,
                w_in, w_conv, w_pa, w_pb, w_o, ln1_g, ln1_b,
                w_router, b_router, w1, b1, w2, b2, ln2_g, ln2_b):
    b, t, _ = x.shape
    q_pos = past_len + jnp.arange(t, dtype=jnp.int32)
    k_pos = jnp.arange(past_len + t, dtype=jnp.int32)
    h = jnp.dot(x, w_in)
    cuts = np.cumsum(IN_SIZES)[:-1].tolist()
    q, k, v, qi, ki, wi, cb, cc, cx, ga, gb = jnp.split(h, cuts, axis=-1)
    q = rope(q.reshape(b, t, N_HEADS, HEAD_DIM), q_pos)
    k = rope(k.reshape(b, t, KV_HEADS, HEAD_DIM), q_pos)
    v = v.reshape(b, t, KV_HEADS, HEAD_DIM)
    qi = rope(qi.reshape(b, t, IDX_HEADS, IDX_DIM), q_pos)
    ki = rope(ki[:, :, None, :], q_pos)[:, :, 0, :]
    wi = wi * IDX_HEADS ** -0.5
    if k_past is None:
        k_all, v_all, ki_all = k, v, ki
    else:
        k_all = jnp.concatenate([k_past.astype(k.dtype), k], axis=1)
        v_all = jnp.concatenate([v_past.astype(v.dtype), v], axis=1)
        ki_all = jnp.concatenate([ki_past.astype(ki.dtype), ki], axis=1)
    n_keys = past_len + t
    topk = min(TOPK_MAX, n_keys // 4)
    a = sparse_attention(q, qi, wi, q_pos, k_all, v_all, ki_all, k_pos, topk)
    u = cc * cx
    u_full = jnp.concatenate([conv_hist.astype(u.dtype), u], axis=1)
    conv = sum(w_conv[j] * u_full[:, j:j + t] for j in range(CONV_W))
    yc = cb * conv
    conv_new = u_full[:, -(CONV_W - 1):]
    merged = jax.nn.sigmoid(ga) * jnp.dot(a, w_pa) + jax.nn.sigmoid(gb) * jnp.dot(yc, w_pb)
    mix = jnp.dot(merged, w_o)
    x = layer_norm(DEEPNORM_ALPHA * x + mix, ln1_g, ln1_b)
    x = layer_norm(DEEPNORM_ALPHA * x + moe_ffn(x, w_router, b_router, w1, b1, w2, b2), ln2_g, ln2_b)
    return x, k, v, ki, conv_new


def setup_inputs(seed: int = 0) -> dict:
    key = jax.random.key(seed)
    ks = jax.random.split(key, 24)

    def nrm(k, shape, s):
        return jax.random.normal(k, shape, jnp.float32) * s

    n_pages = PAST_LEN // PAGE_SIZE
    n_used = DEC_BATCH * n_pages
    n_pool = n_used + (n_used + 3) // 4
    page_table = jax.random.permutation(ks[6], n_pool)[:n_used].reshape(DEC_BATCH, n_pages).astype(jnp.int32)
    return {
        'x_prompt': nrm(ks[0], (BATCH, SEQ, D_MODEL), 1.0),
        'x_sample': nrm(ks[1], (DEC_BATCH, DEC_SEQ, D_MODEL), 1.0),
        'cache_k': nrm(ks[2], (DEPTH, n_pool, PAGE_SIZE, KV_HEADS, HEAD_DIM), 1.0),
        'cache_v': nrm(ks[3], (DEPTH, n_pool, PAGE_SIZE, KV_HEADS, HEAD_DIM), 1.0),
        'cache_kidx': nrm(ks[4], (DEPTH, n_pool, PAGE_SIZE, IDX_DIM), 1.0),
        'state_conv': nrm(ks[5], (DEPTH, DEC_BATCH, CONV_W - 1, CONV_DIM), 1.0),
        'page_table': page_table,
        'w_in': nrm(ks[7], (DEPTH, D_MODEL, IN_COLS), D_MODEL ** -0.5),
        'w_conv': nrm(ks[8], (DEPTH, CONV_W, CONV_DIM), CONV_W ** -0.5),
        'w_pa': nrm(ks[9], (DEPTH, ATTN_WIDTH, D_MODEL), ATTN_WIDTH ** -0.5),
        'w_pb': nrm(ks[10], (DEPTH, CONV_DIM, D_MODEL), CONV_DIM ** -0.5),
        'w_o': nrm(ks[11], (DEPTH, D_MODEL, D_MODEL), D_MODEL ** -0.5 * DEEPNORM_BETA),
        'ln1_g': 1.0 + nrm(ks[12], (DEPTH, D_MODEL), 0.05),
        'ln1_b': nrm(ks[13], (DEPTH, D_MODEL), 0.02),
        'w_router': nrm(ks[14], (DEPTH, D_MODEL, N_EXPERTS), D_MODEL ** -0.5),
        'b_router': nrm(ks[15], (DEPTH, N_EXPERTS), 0.01),
        'w1': nrm(ks[16], (DEPTH, N_EXPERTS, D_MODEL, 2 * D_EXPERT), D_MODEL ** -0.5),
        'b1': nrm(ks[17], (DEPTH, N_EXPERTS, 2 * D_EXPERT), 0.02),
        'w2': nrm(ks[18], (DEPTH, N_EXPERTS, D_EXPERT, D_MODEL), D_EXPERT ** -0.5 * DEEPNORM_BETA),
        'b2': nrm(ks[19], (DEPTH, N_EXPERTS, D_MODEL), 0.02),
        'ln2_g': 1.0 + nrm(ks[20], (DEPTH, D_MODEL), 0.05),
        'ln2_b': nrm(ks[21], (DEPTH, D_MODEL), 0.02),
    }


def reference(x_prompt, x_sample, cache_k, cache_v, cache_kidx, state_conv, page_table,
              w_in, w_conv, w_pa, w_pb, w_o, ln1_g, ln1_b,
              w_router, b_router, w1, b1, w2, b2, ln2_g, ln2_b):
    dec_b, n_pages = page_table.shape
    past_len = n_pages * PAGE_SIZE
    xp, xs = x_prompt, x_sample
    kp_l, vp_l, kip_l, cp_l = [], [], [], []
    ks_l, vs_l, kis_l, cs_l = [], [], [], []
    for l in range(DEPTH):
        lw = (w_in[l], w_conv[l], w_pa[l], w_pb[l], w_o[l], ln1_g[l], ln1_b[l],
              w_router[l], b_router[l], w1[l], b1[l], w2[l], b2[l], ln2_g[l], ln2_b[l])
        hist0 = jnp.zeros((xp.shape[0], CONV_W - 1, CONV_DIM), xp.dtype)
        xp, kn, vn, kin, cn = trunk_layer(xp, 0, None, None, None, hist0, *lw)
        kp_l.append(kn); vp_l.append(vn); kip_l.append(kin); cp_l.append(cn)
        k_past = cache_k[l, page_table].reshape(dec_b, past_len, KV_HEADS, HEAD_DIM)
        v_past = cache_v[l, page_table].reshape(dec_b, past_len, KV_HEADS, HEAD_DIM)
        ki_past = cache_kidx[l, page_table].reshape(dec_b, past_len, IDX_DIM)
        xs, kn, vn, kin, cn = trunk_layer(xs, past_len, k_past, v_past, ki_past, state_conv[l], *lw)
        ks_l.append(kn); vs_l.append(vn); kis_l.append(kin); cs_l.append(cn)
    new_k_p = jnp.stack(kp_l, axis=0)
    new_v_p = jnp.stack(vp_l, axis=0)
    new_kidx_p = jnp.stack(kip_l, axis=0)
    new_conv_p = jnp.stack(cp_l, axis=0)
    new_k_s = jnp.stack(ks_l, axis=0)
    new_v_s = jnp.stack(vs_l, axis=0)
    new_kidx_s = jnp.stack(kis_l, axis=0)
    new_conv_s = jnp.stack(cs_l, axis=0)
    return (xp, xs, new_k_p, new_v_p, new_kidx_p, new_conv_p, new_k_s, new_v_s, new_kidx_s, new_conv_s)
```

```python
import functools

import numpy as np
import jax
import jax.numpy as jnp
from jax import lax
from jax.experimental import pallas as pl
from jax.experimental.pallas import tpu as pltpu

F32 = jnp.float32
BF16 = jnp.bfloat16
I32 = jnp.int32

D_MODEL = 1024
BATCH = 4
SEQ = 4096
DEPTH = 4
DEC_BATCH = 128
DEC_SEQ = 8
PAGE_SIZE = 128
N_PAGES = 16
PAST_LEN = N_PAGES * PAGE_SIZE
N_HEADS = 8
HEAD_DIM = 64
KV_HEADS = 4
IDX_HEADS = 4
IDX_DIM = 64
TOPK = 256
CONV_DIM = 512
N_EXPERTS = 32
TOP_K = 4
D_EXPERT = 1024
SWIGLU_LIMIT = 7.0
SWIGLU_ALPHA = 1.702
ROPE_THETA = 10000.0
LN_EPS = 1e-5
ATTN_WIDTH = N_HEADS * HEAD_DIM
KV_WIDTH = KV_HEADS * HEAD_DIM
IDX_WIDTH = IDX_HEADS * IDX_DIM
DEEPNORM_ALPHA = (2 * DEPTH) ** 0.25

N_P = BATCH * SEQ
N_S = DEC_BATCH * DEC_SEQ
N_TOK = N_P + N_S

LANES = 128
VMEM_LIMIT = 56 * 1024 * 1024
INT_MIN = -2 ** 31

_SEC = {}
_off = 0
for _name, _w in (("q", 512), ("qr", 512), ("k", 256), ("kr", 256), ("v", 256), ("qi", 256), ("qir", 256),
                  ("ki", 256), ("kir", 256), ("wi", 128), ("cb", 512), ("cc", 512), ("cx", 512)):
    _SEC[_name] = (_off, _off + _w)
    _off += _w
W1_COLS = _off

TM = 512
TQ = 256
MOE_BM = 256
MOE_ROWS = N_TOK * TOP_K + N_EXPERTS * MOE_BM
MOE_NBLK = MOE_ROWS // MOE_BM


def _cparams(n_axes):
    return pltpu.CompilerParams(dimension_semantics=("arbitrary",) * n_axes, vmem_limit_bytes=VMEM_LIMIT)


def _const_spec(shape):
    nd = len(shape)
    return pl.BlockSpec(shape, lambda *_: (0,) * nd)


def _inproj_kernel(x_ref, w_ref, cos_ref, sin_ref,
                   q_ref, qi_ref, wi_ref, k_ref, v_ref, ki_ref, kb_ref, vb_ref, kib_ref, cb_ref, u_ref):
    xb = x_ref[...].astype(BF16)

    def mm(name):
        a, b = _SEC[name]
        return jnp.dot(xb, w_ref[:, a:b], preferred_element_type=F32)

    cos = cos_ref[...]
    sin = sin_ref[...]
    cos2 = jnp.concatenate([cos, cos], axis=1)
    sin2 = jnp.concatenate([sin, sin], axis=1)
    cos4 = jnp.concatenate([cos2, cos2], axis=1)
    sin4 = jnp.concatenate([sin2, sin2], axis=1)
    q_ref[...] = mm("q") * cos4 + mm("qr") * sin4
    qi_ref[...] = mm("qi") * cos2 + mm("qir") * sin2
    k = mm("k") * cos2 + mm("kr") * sin2
    k_ref[...] = k
    kb_ref[...] = k.astype(BF16)
    ki = mm("ki") * cos2 + mm("kir") * sin2
    ki_ref[...] = ki
    kib_ref[...] = ki.astype(BF16)
    v = mm("v")
    v_ref[...] = v
    vb_ref[...] = v.astype(BF16)
    wi_ref[...] = mm("wi")
    cb_ref[...] = mm("cb")
    u_ref[...] = mm("cc") * mm("cx")


def _inproj(x, w, cos, sin):
    n = x.shape[0]
    row = lambda w_: pl.BlockSpec((TM, w_), lambda i: (i, 0))
    widths = (512, 256, 128, 256, 256, 256, 256, 256, 256, 512, 512)
    dtypes = (F32, F32, F32, F32, F32, F32, BF16, BF16, BF16, F32, F32)
    return pl.pallas_call(
        _inproj_kernel,
        grid=(n // TM,),
        in_specs=[row(D_MODEL), _const_spec((D_MODEL, W1_COLS)), row(LANES), row(LANES)],
        out_specs=[row(w_) for w_ in widths],
        out_shape=[jax.ShapeDtypeStruct((n, w_), d_) for w_, d_ in zip(widths, dtypes)],
        compiler_params=_cparams(1),
        name="inproj",
    )(x, w, cos, sin)


def _sort_key(score, valid):
    bits = pltpu.bitcast(score, I32)
    key = jnp.where(bits < 0, bits ^ jnp.int32(0x7FFFFFFF), bits)
    key = jnp.where(score == 0.0, jnp.int32(0), key)
    return jnp.where(valid, key, jnp.int32(INT_MIN))


def _count(mask):
    return jnp.sum(jnp.where(mask, 1.0, 0.0), axis=1, keepdims=True)


def _select_topk(key_ref, c0_ref, s_idx, n_keys):
    rows = key_ref.shape[0]
    kf = float(TOPK)

    def thr_step(it, t):
        cand = t + jnp.left_shift(jnp.int32(1), 31 - it)
        cnt = _count(key_ref[...] >= cand)
        return jnp.where(cnt >= kf, cand, t)

    thr = lax.fori_loop(0, 32, thr_step, jnp.full((rows, 1), INT_MIN, I32))
    key = key_ref[...]
    live = key > jnp.int32(INT_MIN)
    n_gt = _count(key > thr)
    n_ge = _count(jnp.logical_and(key >= thr, live))
    c0_ref[...] = jnp.full(c0_ref.shape, n_keys, I32)

    @pl.when(jnp.max(n_ge) > kf)
    def _():
        need = kf - n_gt
        nbits = int(n_keys).bit_length()

        def idx_step(it, c):
            cand = c + jnp.left_shift(jnp.int32(1), nbits - 1 - it)
            eq = jnp.logical_and(key_ref[...] == thr, s_idx < cand)
            return jnp.where(_count(eq) < need, cand, c)

        c_last = lax.fori_loop(0, nbits, idx_step, jnp.zeros((rows, 1), I32))
        c0_ref[...] = jnp.broadcast_to(c_last, c0_ref.shape)

    c0 = c0_ref[:, 0:1]
    tie = jnp.logical_and(key == thr, s_idx <= c0)
    return jnp.logical_and(jnp.logical_or(key > thr, tie), live)


def _head_mask(x, g):
    lane = lax.broadcasted_iota(I32, x.shape, 1)
    return jnp.where(jnp.right_shift(lane, 6) == g, x, 0.0)


_NT = (((1,), (1,)), ((), ()))


def _attn_p_kernel(q_ref, qi_ref, wi_ref, k_ref, v_ref, ki_ref, o_ref, key_ref, neg_ref, c0_ref, *, n_keys, q_base):
    t0 = q_base + pl.program_id(1) * TQ
    qi = qi_ref[...]
    wi = wi_ref[...]
    kib = ki_ref[...]
    score = jnp.zeros((TQ, n_keys), F32)
    for h in range(IDX_HEADS):
        rel = lax.dot_general(_head_mask(qi, h).astype(BF16), kib, _NT, preferred_element_type=F32)
        score = score + wi[:, h:h + 1] * jnp.maximum(rel, 0.0)
    s_idx = lax.broadcasted_iota(I32, (TQ, n_keys), 1)
    t_idx = t0 + lax.broadcasted_iota(I32, (TQ, n_keys), 0)
    key_ref[...] = _sort_key(score, s_idx <= t_idx)
    sel = _select_topk(key_ref, c0_ref, s_idx, n_keys)
    neg_ref[...] = jnp.where(sel, 0.0, -jnp.inf)

    kb = k_ref[...]
    vb = v_ref[...]
    for hq in range(2):
        qh = q_ref[:, hq * KV_WIDTH:(hq + 1) * KV_WIDTH]
        acc = jnp.zeros((TQ, KV_WIDTH), F32)
        for g in range(KV_HEADS):
            lg = lax.dot_general(_head_mask(qh, g).astype(BF16), kb, _NT, preferred_element_type=F32)
            lg = lg * HEAD_DIM ** -0.5 + neg_ref[...]
            p = jnp.exp(lg - jnp.max(lg, axis=1, keepdims=True))
            inv = 1.0 / jnp.sum(p, axis=1, keepdims=True)
            pv = jnp.dot(p.astype(BF16), vb, preferred_element_type=F32)
            acc = acc + _head_mask(pv * inv, g)
        o_ref[:, hq * KV_WIDTH:(hq + 1) * KV_WIDTH] = acc


def _attn_p_group(q, qi, wi, kb, vb, kib, q_lo, q_hi, n_keys):
    nq = (q_hi - q_lo) // TQ
    qrow = lambda w_: pl.BlockSpec((TQ, w_), lambda b, i: ((b * SEQ + q_lo) // TQ + i, 0))
    krow = pl.BlockSpec((n_keys, KV_WIDTH), lambda b, i: (b * (SEQ // n_keys), 0))
    out = pl.pallas_call(
        functools.partial(_attn_p_kernel, n_keys=n_keys, q_base=q_lo),
        grid=(BATCH, nq),
        in_specs=[qrow(ATTN_WIDTH), qrow(IDX_WIDTH), qrow(LANES), krow, krow, krow],
        out_specs=pl.BlockSpec((TQ, ATTN_WIDTH), lambda b, i: (b * nq + i, 0)),
        out_shape=jax.ShapeDtypeStruct((BATCH * (q_hi - q_lo), ATTN_WIDTH), F32),
        scratch_shapes=[pltpu.VMEM((TQ, n_keys), I32), pltpu.VMEM((TQ, n_keys), F32), pltpu.VMEM((TQ, LANES), I32)],
        compiler_params=_cparams(2),
        name=f"attn_p{n_keys}",
    )(q, qi, wi, kb, vb, kib)
    return out.reshape(BATCH, q_hi - q_lo, ATTN_WIDTH)


_P_GROUPS = ((0, 1024, 1024), (1024, 2048, 2048), (2048, 4096, 4096))


def _attn_prompt(q, qi, wi, kb, vb, kib):
    parts = [_attn_p_group(q, qi, wi, kb, vb, kib, lo, hi, nk) for lo, hi, nk in _P_GROUPS]
    return jnp.concatenate(parts, axis=1).reshape(N_P, ATTN_WIDTH)


S_KEYS = PAST_LEN + LANES


def _attn_s_kernel(pt_ref, q_ref, qi_ref, wi_ref, kn_ref, vn_ref, kin_ref, *rest):
    kT = rest[0:N_PAGES]
    vT = rest[N_PAGES:2 * N_PAGES]
    kiT = rest[2 * N_PAGES:3 * N_PAGES]
    o_ref, kTs, vTs, kiTs, key_ref, c0_ref = rest[3 * N_PAGES:]
    del pt_ref
    for j in range(N_PAGES):
        sl = slice(j * PAGE_SIZE, (j + 1) * PAGE_SIZE)
        kTs[:, sl] = kT[j][...].astype(BF16)
        vTs[:, sl] = vT[j][...].astype(BF16)
        kiTs[:, sl] = kiT[j][...].astype(BF16)

    pad = jnp.zeros((LANES - DEC_SEQ, KV_WIDTH), F32)
    kn = jnp.concatenate([kn_ref[...], pad], axis=0).astype(BF16)
    vn = jnp.concatenate([vn_ref[...], pad], axis=0).astype(BF16)
    kin = jnp.concatenate([kin_ref[...], pad], axis=0)[:, :IDX_DIM].astype(BF16)

    qi = qi_ref[...]
    wi = wi_ref[...]
    qis = jnp.concatenate([qi[:, h * IDX_DIM:(h + 1) * IDX_DIM] for h in range(IDX_HEADS)], axis=0).astype(BF16)
    rel = jnp.concatenate([jnp.dot(qis, kiTs[...], preferred_element_type=F32),
                           lax.dot_general(qis, kin, _NT, preferred_element_type=F32)], axis=1)
    rel = jnp.maximum(rel, 0.0)
    score = jnp.zeros((DEC_SEQ, S_KEYS), F32)
    for h in range(IDX_HEADS):
        score = score + wi[:, h:h + 1] * rel[h * DEC_SEQ:(h + 1) * DEC_SEQ]
    s_idx = lax.broadcasted_iota(I32, (DEC_SEQ, S_KEYS), 1)
    t_idx = PAST_LEN + lax.broadcasted_iota(I32, (DEC_SEQ, S_KEYS), 0)
    key_ref[...] = _sort_key(score, s_idx <= t_idx)
    sel = _select_topk(key_ref, c0_ref, s_idx, S_KEYS)
    neg = jnp.where(sel, 0.0, -jnp.inf)

    q = q_ref[...]
    qs = jnp.concatenate([_head_mask(q[:, hq * KV_WIDTH:(hq + 1) * KV_WIDTH], g)
                          for hq in range(2) for g in range(KV_HEADS)], axis=0).astype(BF16)
    lg = jnp.concatenate([jnp.dot(qs, kTs[...], preferred_element_type=F32),
                          lax.dot_general(qs, kn, _NT, preferred_element_type=F32)], axis=1)
    lg = lg * HEAD_DIM ** -0.5 + jnp.concatenate([neg] * N_HEADS, axis=0)
    p = jnp.exp(lg - jnp.max(lg, axis=1, keepdims=True))
    inv = 1.0 / jnp.sum(p, axis=1, keepdims=True)
    pb = p.astype(BF16)
    pv = (lax.dot_general(pb[:, :PAST_LEN], vTs[...], _NT, preferred_element_type=F32)
          + jnp.dot(pb[:, PAST_LEN:], vn, preferred_element_type=F32)) * inv
    for hq in range(2):
        acc = jnp.zeros((DEC_SEQ, KV_WIDTH), F32)
        for g in range(KV_HEADS):
            r = (hq * KV_HEADS + g) * DEC_SEQ
            acc = acc + _head_mask(pv[r:r + DEC_SEQ], g)
        o_ref[:, hq * KV_WIDTH:(hq + 1) * KV_WIDTH] = acc


def _attn_sample(layer, page_table, q, qi, wi, k, v, ki, ckT, cvT, ckiT):
    srow = lambda w_: pl.BlockSpec((DEC_SEQ, w_), lambda b, pt: (N_P // DEC_SEQ + b, 0))

    def page(rows, j):
        return pl.BlockSpec((None, None, rows, PAGE_SIZE), lambda b, pt: (layer, pt[b * N_PAGES + j], 0, 0))

    in_specs = [srow(ATTN_WIDTH), srow(IDX_WIDTH), srow(LANES), srow(KV_WIDTH), srow(KV_WIDTH), srow(KV_WIDTH)]
    in_specs += [page(KV_WIDTH, j) for j in range(N_PAGES)]
    in_specs += [page(KV_WIDTH, j) for j in range(N_PAGES)]
    in_specs += [page(IDX_DIM, j) for j in range(N_PAGES)]
    grid_spec = pltpu.PrefetchScalarGridSpec(
        num_scalar_prefetch=1,
        grid=(DEC_BATCH,),
        in_specs=in_specs,
        out_specs=pl.BlockSpec((DEC_SEQ, ATTN_WIDTH), lambda b, pt: (b, 0)),
        scratch_shapes=[pltpu.VMEM((KV_WIDTH, PAST_LEN), BF16), pltpu.VMEM((KV_WIDTH, PAST_LEN), BF16),
                        pltpu.VMEM((IDX_DIM, PAST_LEN), BF16), pltpu.VMEM((DEC_SEQ, S_KEYS), I32),
                        pltpu.VMEM((DEC_SEQ, LANES), I32)],
    )
    return pl.pallas_call(
        _attn_s_kernel,
        grid_spec=grid_spec,
        out_shape=jax.ShapeDtypeStruct((N_S, ATTN_WIDTH), F32),
        compiler_params=_cparams(1),
        name="attn_s",
    )(page_table, q, qi, wi, k, v, ki, *([ckT] * N_PAGES), *([cvT] * N_PAGES), *([ckiT] * N_PAGES))


def _layer_norm(y, g, b):
    mu = jnp.mean(y, axis=1, keepdims=True)
    d = y - mu
    var = jnp.mean(d * d, axis=1, keepdims=True)
    return d * lax.rsqrt(var + LN_EPS) * g + b


def _merge_kernel(x_ref, a_ref, cb_ref, u_ref, um1_ref, um2_ref, wg_ref, wpa_ref, wpb_ref, wo_ref, wc_ref,
                  g_ref, b_ref, wr_ref, br_ref, x1_ref, x1b_ref, te_ref, tg_ref):
    x = x_ref[...]
    gates = jax.nn.sigmoid(jnp.dot(x.astype(BF16), wg_ref[...], preferred_element_type=F32))
    wc = wc_ref[...]
    conv = wc[0:1] * um2_ref[...] + wc[1:2] * um1_ref[...] + wc[2:3] * u_ref[...]
    yc = cb_ref[...] * conv
    pa = jnp.dot(a_ref[...].astype(BF16), wpa_ref[...], preferred_element_type=F32)
    pb = jnp.dot(yc.astype(BF16), wpb_ref[...], preferred_element_type=F32)
    merged = gates[:, :D_MODEL] * pa + gates[:, D_MODEL:] * pb
    mix = jnp.dot(merged.astype(BF16), wo_ref[...], preferred_element_type=F32)
    x1 = _layer_norm(DEEPNORM_ALPHA * x + mix, g_ref[...], b_ref[...])
    x1_ref[...] = x1
    x1b = x1.astype(BF16)
    x1b_ref[...] = x1b

    lg = jnp.dot(x1b, wr_ref[...], preferred_element_type=F32) + br_ref[...]
    lane = lax.broadcasted_iota(I32, lg.shape, 1)
    te = jnp.zeros(lg.shape, I32)
    vals = []
    for r in range(TOP_K):
        m = jnp.max(lg, axis=1, keepdims=True)
        idx = jnp.min(jnp.where(lg == m, lane, LANES), axis=1, keepdims=True)
        te = jnp.where(lane == r, idx, te)
        vals.append(m)
        lg = jnp.where(lane == idx, -jnp.inf, lg)
    ex = [jnp.exp(v_ - vals[0]) for v_ in vals]
    inv = 1.0 / (ex[0] + ex[1] + ex[2] + ex[3])
    tg = jnp.zeros(lg.shape, F32)
    for r in range(TOP_K):
        tg = jnp.where(lane == r, ex[r] * inv, tg)
    te_ref[...] = te
    tg_ref[...] = tg


def _merge(x, a, cb, u, um1, um2, wg, wpa, wpb, wo, wc, g, b, wr, br):
    n = x.shape[0]
    row = lambda w_: pl.BlockSpec((TM, w_), lambda i: (i, 0))
    return pl.pallas_call(
        _merge_kernel,
        grid=(n // TM,),
        in_specs=[row(D_MODEL), row(ATTN_WIDTH), row(CONV_DIM), row(CONV_DIM), row(CONV_DIM), row(CONV_DIM),
                  _const_spec(wg.shape), _const_spec(wpa.shape), _const_spec(wpb.shape), _const_spec(wo.shape),
                  _const_spec(wc.shape), _const_spec(g.shape), _const_spec(b.shape),
                  _const_spec(wr.shape), _const_spec(br.shape)],
        out_specs=[row(D_MODEL), row(D_MODEL), row(LANES), row(LANES)],
        out_shape=[jax.ShapeDtypeStruct((n, D_MODEL), F32), jax.ShapeDtypeStruct((n, D_MODEL), BF16),
                   jax.ShapeDtypeStruct((n, LANES), I32), jax.ShapeDtypeStruct((n, LANES), F32)],
        compiler_params=_cparams(1),
        name="merge",
    )(x, a, cb, u, um1, um2, wg, wpa, wpb, wo, wc, g, b, wr, br)


def _moe_kernel(be_ref, na_ref, xs_ref, g_ref, w1g_ref, w1l_ref, b1g_ref, b1l_ref, w2_ref, b2_ref, o_ref):
    del be_ref
    i = pl.program_id(0)

    @pl.when(i < na_ref[0])
    def _():
        xs = xs_ref[...]
        hg = jnp.dot(xs, w1g_ref[...], preferred_element_type=F32) + b1g_ref[...]
        hl = jnp.dot(xs, w1l_ref[...], preferred_element_type=F32) + b1l_ref[...]
        glu = jnp.minimum(hg, SWIGLU_LIMIT)
        lin = jnp.clip(hl, -SWIGLU_LIMIT, SWIGLU_LIMIT)
        act = glu * jax.nn.sigmoid(SWIGLU_ALPHA * glu) * (lin + 1.0)
        y = jnp.dot(act.astype(BF16), w2_ref[...], preferred_element_type=F32) + b2_ref[...]
        o_ref[...] = y * g_ref[...]

    @pl.when(i >= na_ref[0])
    def _():
        o_ref[...] = jnp.zeros(o_ref.shape, F32)


def _moe(blk_e, n_act, xs, row_g, w1g, w1l, b1g, b1l, w2, b2):
    wspec = lambda r, c: pl.BlockSpec((None, r, c), lambda i, be, na: (be[i], 0, 0))
    grid_spec = pltpu.PrefetchScalarGridSpec(
        num_scalar_prefetch=2,
        grid=(MOE_NBLK,),
        in_specs=[pl.BlockSpec((MOE_BM, D_MODEL), lambda i, be, na: (i, 0)),
                  pl.BlockSpec((MOE_BM, 1), lambda i, be, na: (i, 0)),
                  wspec(D_MODEL, D_EXPERT), wspec(D_MODEL, D_EXPERT), wspec(1, D_EXPERT), wspec(1, D_EXPERT),
                  wspec(D_EXPERT, D_MODEL), wspec(1, D_MODEL)],
        out_specs=pl.BlockSpec((MOE_BM, D_MODEL), lambda i, be, na: (i, 0)),
    )
    return pl.pallas_call(
        _moe_kernel,
        grid_spec=grid_spec,
        out_shape=jax.ShapeDtypeStruct((MOE_ROWS, D_MODEL), F32),
        compiler_params=_cparams(1),
        name="moe",
    )(blk_e, n_act, xs, row_g, w1g, w1l, b1g, b1l, w2, b2)


def _route(top_e, top_g):
    flat_e = top_e.reshape(-1)
    onehot = (flat_e[:, None] == jnp.arange(N_EXPERTS, dtype=I32)[None, :]).astype(I32)
    csum = jnp.cumsum(onehot, axis=0)
    counts = csum[-1]
    rank = jnp.take_along_axis(csum, flat_e[:, None], axis=1)[:, 0] - 1
    padded = (counts + MOE_BM - 1) // MOE_BM * MOE_BM
    pad_end = jnp.cumsum(padded)
    pad_start = pad_end - padded
    dest = pad_start[flat_e] + rank
    flat_tok = jnp.repeat(jnp.arange(N_TOK, dtype=I32), TOP_K)
    row_tok = jnp.zeros((MOE_ROWS,), I32).at[dest].set(flat_tok)
    row_g = jnp.zeros((MOE_ROWS,), F32).at[dest].set(top_g.reshape(-1))
    blk_start = jnp.arange(MOE_NBLK, dtype=I32) * MOE_BM
    blk_e = jnp.minimum(jnp.sum(blk_start[:, None] >= pad_end[None, :], axis=1), N_EXPERTS - 1).astype(I32)
    n_act = (pad_end[-1] // MOE_BM).astype(I32).reshape(1)
    return dest, row_tok, row_g, blk_e, n_act


def _ln2_kernel(x_ref, y_ref, g_ref, b_ref, o_ref):
    y = y_ref[0] + y_ref[1] + y_ref[2] + y_ref[3]
    o_ref[...] = _layer_norm(DEEPNORM_ALPHA * x_ref[...] + y, g_ref[...], b_ref[...])


def _ln2(x1, yk, g, b):
    n = x1.shape[0]
    return pl.pallas_call(
        _ln2_kernel,
        grid=(n // TM,),
        in_specs=[pl.BlockSpec((TM, D_MODEL), lambda i: (i, 0)),
                  pl.BlockSpec((TOP_K, TM, D_MODEL), lambda i: (0, i, 0)),
                  _const_spec(g.shape), _const_spec(b.shape)],
        out_specs=pl.BlockSpec((TM, D_MODEL), lambda i: (i, 0)),
        out_shape=jax.ShapeDtypeStruct((n, D_MODEL), F32),
        compiler_params=_cparams(1),
        name="ln2",
    )(x1, yk, g, b)


def _rot_cols(w, n_heads):
    lead = w.shape[:-1]
    w4 = w.reshape(*lead, n_heads, 2, HEAD_DIM // 2)
    return jnp.concatenate([-w4[..., 1, :], w4[..., 0, :]], axis=-1).reshape(*lead, n_heads * HEAD_DIM)


def _pair_major(w):
    lead = w.shape[:-1]
    w4 = w.reshape(*lead, KV_HEADS, 2, HEAD_DIM)
    return jnp.swapaxes(w4, -3, -2).reshape(*lead, ATTN_WIDTH)


def _prep_weights(w_in, w_pa, w_router, b_router, w1, w2):
    c = np.cumsum((ATTN_WIDTH, KV_WIDTH, KV_WIDTH, IDX_WIDTH, IDX_DIM, IDX_HEADS,
                   CONV_DIM, CONV_DIM, CONV_DIM, D_MODEL, D_MODEL)).tolist()
    wq, wk, wv = w_in[..., :c[0]], w_in[..., c[0]:c[1]], w_in[..., c[1]:c[2]]
    wqi, wki, wwi = w_in[..., c[2]:c[3]], w_in[..., c[3]:c[4]], w_in[..., c[4]:c[5]]
    wconv = w_in[..., c[5]:c[8]]
    wgate = w_in[..., c[8]:]
    wki_r = _rot_cols(wki, 1)
    wwi = jnp.pad(wwi * (IDX_HEADS ** -0.5 * IDX_DIM ** -0.5), ((0, 0), (0, 0), (0, LANES - IDX_HEADS)))
    w1cat = jnp.concatenate(
        [_pair_major(wq), _pair_major(_rot_cols(wq, N_HEADS)), wk, _rot_cols(wk, KV_HEADS), wv,
         wqi, _rot_cols(wqi, IDX_HEADS), jnp.tile(wki, (1, 1, IDX_HEADS)), jnp.tile(wki_r, (1, 1, IDX_HEADS)),
         wwi, wconv], axis=-1).astype(BF16)
    wpa = jnp.swapaxes(w_pa.reshape(DEPTH, KV_HEADS, 2, HEAD_DIM, D_MODEL), 1, 2).reshape(DEPTH, ATTN_WIDTH, D_MODEL)
    wr = jnp.pad(w_router, ((0, 0), (0, 0), (0, LANES - N_EXPERTS))).astype(BF16)
    br = jnp.pad(b_router, ((0, 0), (0, LANES - N_EXPERTS)), constant_values=-jnp.inf)[:, None, :]
    w1g = w1[..., 0::2].astype(BF16)
    w1l = w1[..., 1::2].astype(BF16)
    return w1cat, wgate.astype(BF16), wpa.astype(BF16), wr, br, w1g, w1l, w2.astype(BF16)


def _rope_tables():
    pos = jnp.concatenate([jnp.tile(jnp.arange(SEQ, dtype=I32), BATCH),
                           jnp.tile(PAST_LEN + jnp.arange(DEC_SEQ, dtype=I32), DEC_BATCH)])
    half = HEAD_DIM // 2
    inv = ROPE_THETA ** (-jnp.arange(half, dtype=F32) / half)
    ang = pos.astype(F32)[:, None] * inv[None, :]
    return jnp.tile(jnp.cos(ang), (1, LANES // half)), jnp.tile(jnp.sin(ang), (1, LANES // half))


def kernel(x_prompt, x_sample, cache_k, cache_v, cache_kidx, state_conv, page_table, w_in, w_conv, w_pa, w_pb,
           w_o, ln1_g, ln1_b, w_router, b_router, w1, b1, w2, b2, ln2_g, ln2_b):
    w1cat, wgate, wpa, wr, br, w1g, w1l, w2b = _prep_weights(w_in, w_pa, w_router, b_router, w1, w2)
    wpb = w_pb.astype(BF16)
    wo = w_o.astype(BF16)
    wc = jnp.pad(w_conv, ((0, 0), (0, 8 - w_conv.shape[1]), (0, 0)))
    b1g = b1[..., 0::2][:, :, None, :]
    b1l = b1[..., 1::2][:, :, None, :]
    b2r = b2[:, :, None, :]
    cos, sin = _rope_tables()
    ckT = jnp.transpose(cache_k, (0, 1, 3, 4, 2)).reshape(DEPTH, -1, KV_WIDTH, PAGE_SIZE)
    cvT = jnp.transpose(cache_v, (0, 1, 3, 4, 2)).reshape(DEPTH, -1, KV_WIDTH, PAGE_SIZE)
    ckiT = jnp.transpose(cache_kidx, (0, 1, 3, 2))
    pt = page_table.reshape(-1).astype(I32)

    x = jnp.concatenate([x_prompt.reshape(N_P, D_MODEL), x_sample.reshape(N_S, D_MODEL)], axis=0)
    ks, vs, kis, convs = [], [], [], []
    for l in range(DEPTH):
        q, qi, wi, k, v, ki4, kb, vb, kib, cb, u = _inproj(x, w1cat[l], cos, sin)
        a_p = _attn_prompt(q, qi, wi, kb, vb, kib)
        a_s = _attn_sample(l, pt, q, qi, wi, k, v, ki4, ckT, cvT, ckiT)
        a = jnp.concatenate([a_p, a_s], axis=0)

        up = u[:N_P].reshape(BATCH, SEQ, CONV_DIM)
        us = jnp.concatenate([state_conv[l], u[N_P:].reshape(DEC_BATCH, DEC_SEQ, CONV_DIM)], axis=1)
        zp = jnp.zeros((BATCH, 2, CONV_DIM), F32)
        um1 = jnp.concatenate([jnp.concatenate([zp[:, :1], up[:, :-1]], axis=1).reshape(N_P, CONV_DIM),
                               us[:, 1:1 + DEC_SEQ].reshape(N_S, CONV_DIM)], axis=0)
        um2 = jnp.concatenate([jnp.concatenate([zp, up[:, :-2]], axis=1).reshape(N_P, CONV_DIM),
                               us[:, 0:DEC_SEQ].reshape(N_S, CONV_DIM)], axis=0)

        x1, x1b, te, tg = _merge(x, a, cb, u, um1, um2, wgate[l], wpa[l], wpb[l], wo[l], wc[l],
                                 ln1_g[l][None], ln1_b[l][None], wr[l], br[l])
        dest, row_tok, row_g, blk_e, n_act = _route(te[:, :TOP_K], tg[:, :TOP_K])
        xs = jnp.take(x1b, row_tok, axis=0)
        y_rows = _moe(blk_e, n_act, xs, row_g[:, None], w1g[l], w1l[l], b1g[l], b1l[l], w2b[l], b2r[l])
        yk = jnp.take(y_rows, dest.reshape(N_TOK, TOP_K).T, axis=0)
        x = _ln2(x1, yk, ln2_g[l][None], ln2_b[l][None])

        ks.append(k)
        vs.append(v)
        kis.append(ki4[:, :IDX_DIM])
        convs.append((up[:, -2:], us[:, -2:]))

    def split(parts, tail):
        st = jnp.stack(parts, axis=0)
        return (st[:, :N_P].reshape(DEPTH, BATCH, SEQ, *tail), st[:, N_P:].reshape(DEPTH, DEC_BATCH, DEC_SEQ, *tail))

    k_p, k_s = split(ks, (KV_HEADS, HEAD_DIM))
    v_p, v_s = split(vs, (KV_HEADS, HEAD_DIM))
    ki_p, ki_s = split(kis, (IDX_DIM,))
    conv_p = jnp.stack([c_[0] for c_ in convs], axis=0)
    conv_s = jnp.stack([c_[1] for c_ in convs], axis=0)
    y_p = x[:N_P].reshape(BATCH, SEQ, D_MODEL)
    y_s = x[N_P:].reshape(DEC_BATCH, DEC_SEQ, D_MODEL)
    return (y_p, y_s, k_p, v_p, ki_p, conv_p, k_s, v_s, ki_s, conv_s)
```

```python
import functools

import numpy as np
import jax
import jax.numpy as jnp
from jax import lax
from jax.experimental import pallas as pl
from jax.experimental.pallas import tpu as pltpu
from jax.experimental.pallas import tpu_sc as plsc

F32 = jnp.float32
BF16 = jnp.bfloat16
I32 = jnp.int32

D_MODEL = 1024
BATCH = 4
SEQ = 4096
DEPTH = 4
DEC_BATCH = 128
DEC_SEQ = 8
PAGE_SIZE = 128
N_PAGES = 16
PAST_LEN = N_PAGES * PAGE_SIZE
N_HEADS = 8
HEAD_DIM = 64
KV_HEADS = 4
IDX_HEADS = 4
IDX_DIM = 64
TOPK = 256
CONV_DIM = 512
N_EXPERTS = 32
TOP_K = 4
D_EXPERT = 1024
SWIGLU_LIMIT = 7.0
SWIGLU_ALPHA = 1.702
ROPE_THETA = 10000.0
LN_EPS = 1e-5
ATTN_WIDTH = N_HEADS * HEAD_DIM
KV_WIDTH = KV_HEADS * HEAD_DIM
IDX_WIDTH = IDX_HEADS * IDX_DIM
DEEPNORM_ALPHA = (2 * DEPTH) ** 0.25

N_P = BATCH * SEQ
N_S = DEC_BATCH * DEC_SEQ
N_TOK = N_P + N_S

LANES = 128
VMEM_LIMIT = 56 * 1024 * 1024
INT_MIN = -2 ** 31

_SEC = {}
_off = 0
for _name, _w in (("q", 512), ("qr", 512), ("k", 256), ("kr", 256), ("v", 256), ("qi", 256), ("qir", 256),
                  ("ki", 256), ("kir", 256), ("wi", 128), ("cb", 512), ("cc", 512), ("cx", 512)):
    _SEC[_name] = (_off, _off + _w)
    _off += _w
W1_COLS = _off
_SEC_LO = {}
_off = 0
for _name in ("qi", "qir", "ki", "kir", "wi"):
    _w = _SEC[_name][1] - _SEC[_name][0]
    _SEC_LO[_name] = (_off, _off + _w)
    _off += _w
W1_LO_COLS = _off

TM = 512
TQ = 256
MOE_BM = 256
MOE_ROWS = N_TOK * TOP_K + N_EXPERTS * MOE_BM
MOE_NBLK = MOE_ROWS // MOE_BM


def _cparams(n_axes):
    return pltpu.CompilerParams(dimension_semantics=("arbitrary",) * n_axes, vmem_limit_bytes=VMEM_LIMIT)


def _const_spec(shape):
    nd = len(shape)
    return pl.BlockSpec(shape, lambda *_: (0,) * nd)


def _split_hi_lo(x):
    hi = lax.bitcast_convert_type(lax.bitcast_convert_type(x, I32) & jnp.int32(-65536), F32)
    return hi.astype(BF16), (x - hi).astype(BF16)


def _inproj_kernel(x_ref, w_ref, wlo_ref, cos_ref, sin_ref,
                   q_ref, qi_ref, wi_ref, k_ref, v_ref, ki_ref, kb_ref, vb_ref, kib_ref, kil_ref, cb_ref, u_ref):
    x = x_ref[...]
    xb = x.astype(BF16)
    xhi, xlo = _split_hi_lo(x)

    def mm(name):
        a, b = _SEC[name]
        if name not in _SEC_LO:
            return jnp.dot(xb, w_ref[:, a:b], preferred_element_type=F32)
        la, lb = _SEC_LO[name]
        return (jnp.dot(xhi, w_ref[:, a:b], preferred_element_type=F32)
                + jnp.dot(xlo, w_ref[:, a:b], preferred_element_type=F32)
                + jnp.dot(xhi, wlo_ref[:, la:lb], preferred_element_type=F32)
                + jnp.dot(xlo, wlo_ref[:, la:lb], preferred_element_type=F32))

    cos = cos_ref[...]
    sin = sin_ref[...]
    cos2 = jnp.concatenate([cos, cos], axis=1)
    sin2 = jnp.concatenate([sin, sin], axis=1)
    cos4 = jnp.concatenate([cos2, cos2], axis=1)
    sin4 = jnp.concatenate([sin2, sin2], axis=1)
    q_ref[...] = mm("q") * cos4 + mm("qr") * sin4
    qi_ref[...] = mm("qi") * cos2 + mm("qir") * sin2
    k = mm("k") * cos2 + mm("kr") * sin2
    k_ref[...] = k
    kb_ref[...] = k.astype(BF16)
    ki = mm("ki") * cos2 + mm("kir") * sin2
    ki_ref[...] = ki
    kib_ref[...], kil_ref[...] = _split_hi_lo(ki)
    v = mm("v")
    v_ref[...] = v
    vb_ref[...] = v.astype(BF16)
    wi_ref[...] = mm("wi")
    cb_ref[...] = mm("cb")
    u_ref[...] = mm("cc") * mm("cx")


def _inproj(x, w, wlo, cos, sin):
    n = x.shape[0]
    row = lambda w_: pl.BlockSpec((TM, w_), lambda i: (i, 0))
    widths = (512, 256, 128, 256, 256, 256, 256, 256, 256, 256, 512, 512)
    dtypes = (F32, F32, F32, F32, F32, F32, BF16, BF16, BF16, BF16, F32, F32)
    return pl.pallas_call(
        _inproj_kernel,
        grid=(n // TM,),
        in_specs=[row(D_MODEL), _const_spec((D_MODEL, W1_COLS)), _const_spec((D_MODEL, W1_LO_COLS)),
                  row(LANES), row(LANES)],
        out_specs=[row(w_) for w_ in widths],
        out_shape=[jax.ShapeDtypeStruct((n, w_), d_) for w_, d_ in zip(widths, dtypes)],
        compiler_params=_cparams(1),
        name="inproj",
    )(x, w, wlo, cos, sin)


def _sort_key(score, valid):
    bits = pltpu.bitcast(score, I32)
    key = jnp.where(bits < 0, bits ^ jnp.int32(0x7FFFFFFF), bits)
    key = jnp.where(score == 0.0, jnp.int32(0), key)
    return jnp.where(valid, key, jnp.int32(INT_MIN))


def _count(mask):
    return jnp.sum(jnp.where(mask, 1.0, 0.0), axis=1, keepdims=True)


def _select_topk(key_ref, c0_ref, s_idx, n_keys):
    rows = key_ref.shape[0]
    kf = float(TOPK)

    def thr_step(it, t):
        cand = t + jnp.left_shift(jnp.int32(1), 31 - it)
        cnt = _count(key_ref[...] >= cand)
        return jnp.where(cnt >= kf, cand, t)

    thr = lax.fori_loop(0, 32, thr_step, jnp.full((rows, 1), INT_MIN, I32))
    key = key_ref[...]
    live = key > jnp.int32(INT_MIN)
    n_gt = _count(key > thr)
    n_ge = _count(jnp.logical_and(key >= thr, live))
    c0_ref[...] = jnp.full(c0_ref.shape, n_keys, I32)

    @pl.when(jnp.max(n_ge) > kf)
    def _():
        need = kf - n_gt
        nbits = int(n_keys).bit_length()

        def idx_step(it, c):
            cand = c + jnp.left_shift(jnp.int32(1), nbits - 1 - it)
            eq = jnp.logical_and(key_ref[...] == thr, s_idx < cand)
            return jnp.where(_count(eq) < need, cand, c)

        c_last = lax.fori_loop(0, nbits, idx_step, jnp.zeros((rows, 1), I32))
        c0_ref[...] = jnp.broadcast_to(c_last, c0_ref.shape)

    c0 = c0_ref[:, 0:1]
    tie = jnp.logical_and(key == thr, s_idx <= c0)
    return jnp.logical_and(jnp.logical_or(key > thr, tie), live)


def _head_mask(x, g):
    lane = lax.broadcasted_iota(I32, x.shape, 1)
    return jnp.where(jnp.right_shift(lane, 6) == g, x, 0.0)


_NT = (((1,), (1,)), ((), ()))


def _attn_p_kernel(q_ref, qi_ref, wi_ref, k_ref, v_ref, ki_ref, kil_ref, o_ref, key_ref, neg_ref, c0_ref,
                   *, n_keys, q_base):
    t0 = q_base + pl.program_id(1) * TQ
    qi = qi_ref[...]
    wi = wi_ref[...]
    kih = ki_ref[...]
    kil = kil_ref[...]
    score = jnp.zeros((TQ, n_keys), F32)
    for h in range(IDX_HEADS):
        qs = jnp.concatenate(_split_hi_lo(_head_mask(qi, h)), axis=0)
        r_hi = lax.dot_general(qs, kih, _NT, preferred_element_type=F32)
        r_lo = lax.dot_general(qs, kil, _NT, preferred_element_type=F32)
        rel = (r_hi[:TQ] + r_hi[TQ:]) + (r_lo[:TQ] + r_lo[TQ:])
        score = score + wi[:, h:h + 1] * jnp.maximum(rel, 0.0)
    s_idx = lax.broadcasted_iota(I32, (TQ, n_keys), 1)
    t_idx = t0 + lax.broadcasted_iota(I32, (TQ, n_keys), 0)
    key_ref[...] = _sort_key(score, s_idx <= t_idx)
    sel = _select_topk(key_ref, c0_ref, s_idx, n_keys)
    neg_ref[...] = jnp.where(sel, 0.0, -jnp.inf)

    kb = k_ref[...]
    vb = v_ref[...]
    for hq in range(2):
        qh = q_ref[:, hq * KV_WIDTH:(hq + 1) * KV_WIDTH]
        acc = jnp.zeros((TQ, KV_WIDTH), F32)
        for g in range(KV_HEADS):
            qg = (_head_mask(qh, g) * HEAD_DIM ** -0.5).astype(BF16)
            lg = lax.dot_general(qg, kb, _NT, preferred_element_type=F32) + neg_ref[...]
            p = jnp.exp(lg - jnp.max(lg, axis=1, keepdims=True))
            inv = 1.0 / jnp.sum(p, axis=1, keepdims=True)
            pv = jnp.dot(p.astype(BF16), vb, preferred_element_type=F32)
            acc = acc + _head_mask(pv * inv, g)
        o_ref[:, hq * KV_WIDTH:(hq + 1) * KV_WIDTH] = acc


def _attn_p_group(q, qi, wi, kb, vb, kib, kil, q_lo, q_hi, n_keys):
    nq = (q_hi - q_lo) // TQ
    qrow = lambda w_: pl.BlockSpec((TQ, w_), lambda b, i: ((b * SEQ + q_lo) // TQ + i, 0))
    krow = pl.BlockSpec((n_keys, KV_WIDTH), lambda b, i: (b * (SEQ // n_keys), 0))
    out = pl.pallas_call(
        functools.partial(_attn_p_kernel, n_keys=n_keys, q_base=q_lo),
        grid=(BATCH, nq),
        in_specs=[qrow(ATTN_WIDTH), qrow(IDX_WIDTH), qrow(LANES), krow, krow, krow, krow],
        out_specs=pl.BlockSpec((TQ, ATTN_WIDTH), lambda b, i: (b * nq + i, 0)),
        out_shape=jax.ShapeDtypeStruct((BATCH * (q_hi - q_lo), ATTN_WIDTH), F32),
        scratch_shapes=[pltpu.VMEM((TQ, n_keys), I32), pltpu.VMEM((TQ, n_keys), F32), pltpu.VMEM((TQ, LANES), I32)],
        compiler_params=_cparams(2),
        name=f"attn_p{n_keys}",
    )(q, qi, wi, kb, vb, kib, kil)
    return out.reshape(BATCH, q_hi - q_lo, ATTN_WIDTH)


_P_GROUPS = ((0, 1024, 1024), (1024, 2048, 2048), (2048, 4096, 4096))


def _attn_prompt(q, qi, wi, kb, vb, kib, kil):
    parts = [_attn_p_group(q, qi, wi, kb, vb, kib, kil, lo, hi, nk) for lo, hi, nk in _P_GROUPS]
    return jnp.concatenate(parts, axis=1).reshape(N_P, ATTN_WIDTH)


S_KEYS = PAST_LEN + LANES


def _attn_s_kernel(pt_ref, q_ref, qi_ref, wi_ref, kn_ref, vn_ref, kin_ref, *rest):
    kT = rest[0:N_PAGES]
    vT = rest[N_PAGES:2 * N_PAGES]
    kiT = rest[2 * N_PAGES:3 * N_PAGES]
    o_ref, kTs, vTs, kiTs, key_ref, c0_ref = rest[3 * N_PAGES:]
    del pt_ref
    for j in range(N_PAGES):
        sl = slice(j * PAGE_SIZE, (j + 1) * PAGE_SIZE)
        kTs[:, sl] = kT[j][...].astype(BF16)
        vTs[:, sl] = vT[j][...].astype(BF16)
        kiTs[:, sl] = kiT[j][...].astype(BF16)

    pad = jnp.zeros((LANES - DEC_SEQ, KV_WIDTH), F32)
    kn = jnp.concatenate([kn_ref[...], pad], axis=0).astype(BF16)
    vn = jnp.concatenate([vn_ref[...], pad], axis=0).astype(BF16)
    kin = jnp.concatenate([kin_ref[...], pad], axis=0)[:, :IDX_DIM].astype(BF16)

    qi = qi_ref[...]
    wi = wi_ref[...]
    qis = jnp.concatenate([qi[:, h * IDX_DIM:(h + 1) * IDX_DIM] for h in range(IDX_HEADS)], axis=0).astype(BF16)
    rel = jnp.concatenate([jnp.dot(qis, kiTs[...], preferred_element_type=F32),
                           lax.dot_general(qis, kin, _NT, preferred_element_type=F32)], axis=1)
    rel = jnp.maximum(rel, 0.0)
    score = jnp.zeros((DEC_SEQ, S_KEYS), F32)
    for h in range(IDX_HEADS):
        score = score + wi[:, h:h + 1] * rel[h * DEC_SEQ:(h + 1) * DEC_SEQ]
    s_idx = lax.broadcasted_iota(I32, (DEC_SEQ, S_KEYS), 1)
    t_idx = PAST_LEN + lax.broadcasted_iota(I32, (DEC_SEQ, S_KEYS), 0)
    key_ref[...] = _sort_key(score, s_idx <= t_idx)
    sel = _select_topk(key_ref, c0_ref, s_idx, S_KEYS)
    neg = jnp.where(sel, 0.0, -jnp.inf)

    q = q_ref[...]
    qs = jnp.concatenate([_head_mask(q[:, hq * KV_WIDTH:(hq + 1) * KV_WIDTH], g)
                          for hq in range(2) for g in range(KV_HEADS)], axis=0).astype(BF16)
    lg = jnp.concatenate([jnp.dot(qs, kTs[...], preferred_element_type=F32),
                          lax.dot_general(qs, kn, _NT, preferred_element_type=F32)], axis=1)
    lg = lg * HEAD_DIM ** -0.5 + jnp.concatenate([neg] * N_HEADS, axis=0)
    p = jnp.exp(lg - jnp.max(lg, axis=1, keepdims=True))
    inv = 1.0 / jnp.sum(p, axis=1, keepdims=True)
    pb = p.astype(BF16)
    pv = (lax.dot_general(pb[:, :PAST_LEN], vTs[...], _NT, preferred_element_type=F32)
          + jnp.dot(pb[:, PAST_LEN:], vn, preferred_element_type=F32)) * inv
    for hq in range(2):
        acc = jnp.zeros((DEC_SEQ, KV_WIDTH), F32)
        for g in range(KV_HEADS):
            r = (hq * KV_HEADS + g) * DEC_SEQ
            acc = acc + _head_mask(pv[r:r + DEC_SEQ], g)
        o_ref[:, hq * KV_WIDTH:(hq + 1) * KV_WIDTH] = acc


def _attn_sample(layer, page_table, q, qi, wi, k, v, ki, ckT, cvT, ckiT):
    srow = lambda w_: pl.BlockSpec((DEC_SEQ, w_), lambda b, pt: (N_P // DEC_SEQ + b, 0))

    def page(rows, j):
        return pl.BlockSpec((None, None, rows, PAGE_SIZE), lambda b, pt: (layer, pt[b * N_PAGES + j], 0, 0))

    in_specs = [srow(ATTN_WIDTH), srow(IDX_WIDTH), srow(LANES), srow(KV_WIDTH), srow(KV_WIDTH), srow(KV_WIDTH)]
    in_specs += [page(KV_WIDTH, j) for j in range(N_PAGES)]
    in_specs += [page(KV_WIDTH, j) for j in range(N_PAGES)]
    in_specs += [page(IDX_DIM, j) for j in range(N_PAGES)]
    grid_spec = pltpu.PrefetchScalarGridSpec(
        num_scalar_prefetch=1,
        grid=(DEC_BATCH,),
        in_specs=in_specs,
        out_specs=pl.BlockSpec((DEC_SEQ, ATTN_WIDTH), lambda b, pt: (b, 0)),
        scratch_shapes=[pltpu.VMEM((KV_WIDTH, PAST_LEN), BF16), pltpu.VMEM((KV_WIDTH, PAST_LEN), BF16),
                        pltpu.VMEM((IDX_DIM, PAST_LEN), BF16), pltpu.VMEM((DEC_SEQ, S_KEYS), I32),
                        pltpu.VMEM((DEC_SEQ, LANES), I32)],
    )
    return pl.pallas_call(
        _attn_s_kernel,
        grid_spec=grid_spec,
        out_shape=jax.ShapeDtypeStruct((N_S, ATTN_WIDTH), F32),
        compiler_params=_cparams(1),
        name="attn_s",
    )(page_table, q, qi, wi, k, v, ki, *([ckT] * N_PAGES), *([cvT] * N_PAGES), *([ckiT] * N_PAGES))


def _layer_norm(y, g, b):
    mu = jnp.mean(y, axis=1, keepdims=True)
    d = y - mu
    var = jnp.mean(d * d, axis=1, keepdims=True)
    return d * lax.rsqrt(var + LN_EPS) * g + b


def _merge_kernel(x_ref, a_ref, cb_ref, u_ref, um1_ref, um2_ref, wg_ref, wpa_ref, wpb_ref, wo_ref, wc_ref,
                  g_ref, b_ref, wr_ref, wrl_ref, br_ref, x1_ref, te_ref, tg_ref):
    x = x_ref[...]
    gates = jax.nn.sigmoid(jnp.dot(x.astype(BF16), wg_ref[...], preferred_element_type=F32))
    wc = wc_ref[...]
    conv = wc[0:1] * um2_ref[...] + wc[1:2] * um1_ref[...] + wc[2:3] * u_ref[...]
    yc = cb_ref[...] * conv
    pa = jnp.dot(a_ref[...].astype(BF16), wpa_ref[...], preferred_element_type=F32)
    pb = jnp.dot(yc.astype(BF16), wpb_ref[...], preferred_element_type=F32)
    merged = gates[:, :D_MODEL] * pa + gates[:, D_MODEL:] * pb
    mix = jnp.dot(merged.astype(BF16), wo_ref[...], preferred_element_type=F32)
    x1 = _layer_norm(DEEPNORM_ALPHA * x + mix, g_ref[...], b_ref[...])
    x1_ref[...] = x1

    x1h, x1l = _split_hi_lo(x1)
    wrh = wr_ref[...]
    wrl = wrl_ref[...]
    lg = (jnp.dot(x1h, wrh, preferred_element_type=F32) + jnp.dot(x1l, wrh, preferred_element_type=F32)
          + jnp.dot(x1h, wrl, preferred_element_type=F32) + jnp.dot(x1l, wrl, preferred_element_type=F32)
          + br_ref[...])
    lane = lax.broadcasted_iota(I32, lg.shape, 1)
    te = jnp.zeros(lg.shape, I32)
    vals = []
    for r in range(TOP_K):
        m = jnp.max(lg, axis=1, keepdims=True)
        idx = jnp.min(jnp.where(lg == m, lane, LANES), axis=1, keepdims=True)
        te = jnp.where(lane == r, idx, te)
        vals.append(m)
        lg = jnp.where(lane == idx, -jnp.inf, lg)
    ex = [jnp.exp(v_ - vals[0]) for v_ in vals]
    inv = 1.0 / (ex[0] + ex[1] + ex[2] + ex[3])
    tg = jnp.zeros(lg.shape, F32)
    for r in range(TOP_K):
        tg = jnp.where(lane == r, ex[r] * inv, tg)
    te_ref[...] = te
    tg_ref[...] = tg


def _merge(x, a, cb, u, um1, um2, wg, wpa, wpb, wo, wc, g, b, wr, wrl, br):
    n = x.shape[0]
    row = lambda w_: pl.BlockSpec((TM, w_), lambda i: (i, 0))
    return pl.pallas_call(
        _merge_kernel,
        grid=(n // TM,),
        in_specs=[row(D_MODEL), row(ATTN_WIDTH), row(CONV_DIM), row(CONV_DIM), row(CONV_DIM), row(CONV_DIM),
                  _const_spec(wg.shape), _const_spec(wpa.shape), _const_spec(wpb.shape), _const_spec(wo.shape),
                  _const_spec(wc.shape), _const_spec(g.shape), _const_spec(b.shape),
                  _const_spec(wr.shape), _const_spec(wrl.shape), _const_spec(br.shape)],
        out_specs=[row(D_MODEL), row(LANES), row(LANES)],
        out_shape=[jax.ShapeDtypeStruct((n, D_MODEL), F32),
                   jax.ShapeDtypeStruct((n, LANES), I32), jax.ShapeDtypeStruct((n, LANES), F32)],
        compiler_params=_cparams(1),
        name="merge",
    )(x, a, cb, u, um1, um2, wg, wpa, wpb, wo, wc, g, b, wr, wrl, br)


PAIR_W = 2 * LANES


def _moe_kernel(be_ref, na_ref, xs_ref, w1_ref, b1g_ref, b1l_ref, w2_ref, b2_ref, sel_ref, o_ref,
                w1g_s, w1l_s, w2_s):
    i = pl.program_id(0)
    active = i < na_ref[0]
    new_expert = jnp.logical_or(i == 0, be_ref[i] != be_ref[jnp.maximum(i - 1, 0)])

    @pl.when(jnp.logical_and(active, new_expert))
    def _():
        sel = sel_ref[...]
        for c in range(2 * D_EXPERT // PAIR_W):
            wc = w1_ref[:, c * PAIR_W:(c + 1) * PAIR_W].astype(BF16)
            d = jnp.dot(wc, sel, preferred_element_type=F32).astype(BF16)
            w1g_s[:, c * LANES:(c + 1) * LANES] = d[:, :LANES]
            w1l_s[:, c * LANES:(c + 1) * LANES] = d[:, LANES:]
        w2_s[...] = w2_ref[...].astype(BF16)

    @pl.when(active)
    def _():
        xs = xs_ref[...].astype(BF16)
        hg = jnp.dot(xs, w1g_s[...], preferred_element_type=F32) + b1g_ref[...]
        hl = jnp.dot(xs, w1l_s[...], preferred_element_type=F32) + b1l_ref[...]
        glu = jnp.minimum(hg, SWIGLU_LIMIT)
        lin = jnp.clip(hl, -SWIGLU_LIMIT, SWIGLU_LIMIT)
        act = glu * jax.nn.sigmoid(SWIGLU_ALPHA * glu) * (lin + 1.0)
        o_ref[...] = jnp.dot(act.astype(BF16), w2_s[...], preferred_element_type=F32) + b2_ref[...]

    @pl.when(jnp.logical_not(active))
    def _():
        o_ref[...] = jnp.zeros(o_ref.shape, F32)


def _pair_select():
    r = np.arange(PAIR_W)[:, None]
    c = np.arange(PAIR_W)[None, :]
    src = np.where(c < LANES, 2 * c, 2 * (c - LANES) + 1)
    return jnp.asarray(r == src, BF16)


def _moe(blk_e, n_act, xs, w1, b1g, b1l, w2, b2):
    wspec = lambda r, c: pl.BlockSpec((None, r, c), lambda i, be, na: (be[i], 0, 0))
    grid_spec = pltpu.PrefetchScalarGridSpec(
        num_scalar_prefetch=2,
        grid=(MOE_NBLK,),
        in_specs=[pl.BlockSpec((MOE_BM, D_MODEL), lambda i, be, na: (i, 0)),
                  wspec(D_MODEL, 2 * D_EXPERT), wspec(1, D_EXPERT), wspec(1, D_EXPERT),
                  wspec(D_EXPERT, D_MODEL), wspec(1, D_MODEL),
                  pl.BlockSpec((PAIR_W, PAIR_W), lambda i, be, na: (0, 0))],
        out_specs=pl.BlockSpec((MOE_BM, D_MODEL), lambda i, be, na: (i, 0)),
        scratch_shapes=[pltpu.VMEM((D_MODEL, D_EXPERT), BF16), pltpu.VMEM((D_MODEL, D_EXPERT), BF16),
                        pltpu.VMEM((D_EXPERT, D_MODEL), BF16)],
    )
    return pl.pallas_call(
        _moe_kernel,
        grid_spec=grid_spec,
        out_shape=jax.ShapeDtypeStruct((MOE_ROWS, D_MODEL), F32),
        compiler_params=_cparams(1),
        name="moe",
    )(blk_e, n_act, xs, w1, b1g, b1l, w2, b2, _pair_select())


SC_ROWS = 64


def _sc_gather(table, idx):
    n_idx = idx.shape[0]
    width = table.shape[1]
    info = plsc.get_sparse_core_info()
    n_cores = info.num_cores
    n_workers = n_cores * info.num_subcores
    per_worker = n_idx // n_workers
    assert per_worker * n_workers == n_idx and per_worker % SC_ROWS == 0
    mesh = plsc.VectorSubcoreMesh(core_axis_name="c", subcore_axis_name="s")

    @functools.partial(
        pl.kernel, mesh=mesh,
        out_type=jax.ShapeDtypeStruct((n_idx, width), table.dtype),
        scratch_types=[pltpu.VMEM((SC_ROWS,), I32), pltpu.VMEM((SC_ROWS, width), table.dtype),
                       pltpu.SemaphoreType.DMA],
    )
    def gather_kernel(table_hbm, idx_hbm, out_hbm, idx_v, rows_v, sem):
        base = (lax.axis_index("s") * n_cores + lax.axis_index("c")) * per_worker

        @pl.loop(0, per_worker // SC_ROWS)
        def _(j):
            off = base + j * SC_ROWS
            pltpu.sync_copy(idx_hbm.at[pl.ds(off, SC_ROWS)], idx_v)
            pltpu.async_copy(table_hbm.at[idx_v], rows_v, sem).wait()
            pltpu.sync_copy(rows_v, out_hbm.at[pl.ds(off, SC_ROWS)])

    return gather_kernel(table, idx)


def _route(top_e):
    flat_e = top_e.reshape(-1)
    onehot = (flat_e[:, None] == jnp.arange(N_EXPERTS, dtype=I32)[None, :]).astype(I32)
    csum = jnp.cumsum(onehot, axis=0)
    counts = csum[-1]
    rank = jnp.take_along_axis(csum, flat_e[:, None], axis=1)[:, 0] - 1
    padded = (counts + MOE_BM - 1) // MOE_BM * MOE_BM
    pad_end = jnp.cumsum(padded)
    pad_start = pad_end - padded
    dest = pad_start[flat_e] + rank
    flat_tok = jnp.repeat(jnp.arange(N_TOK, dtype=I32), TOP_K)
    row_tok = jnp.zeros((MOE_ROWS,), I32).at[dest].set(flat_tok)
    blk_start = jnp.arange(MOE_NBLK, dtype=I32) * MOE_BM
    blk_e = jnp.minimum(jnp.sum(blk_start[:, None] >= pad_end[None, :], axis=1), N_EXPERTS - 1).astype(I32)
    n_act = (pad_end[-1] // MOE_BM).astype(I32).reshape(1)
    return dest, row_tok, blk_e, n_act


def _ln2_kernel(x_ref, y_ref, tg_ref, g_ref, b_ref, o_ref):
    tg = tg_ref[...]
    y = tg[:, 0:1] * y_ref[0]
    for r in range(1, TOP_K):
        y = y + tg[:, r:r + 1] * y_ref[r]
    o_ref[...] = _layer_norm(DEEPNORM_ALPHA * x_ref[...] + y, g_ref[...], b_ref[...])


def _ln2(x1, yk, tg, g, b):
    n = x1.shape[0]
    return pl.pallas_call(
        _ln2_kernel,
        grid=(n // TM,),
        in_specs=[pl.BlockSpec((TM, D_MODEL), lambda i: (i, 0)),
                  pl.BlockSpec((TOP_K, TM, D_MODEL), lambda i: (0, i, 0)),
                  pl.BlockSpec((TM, LANES), lambda i: (i, 0)),
                  _const_spec(g.shape), _const_spec(b.shape)],
        out_specs=pl.BlockSpec((TM, D_MODEL), lambda i: (i, 0)),
        out_shape=jax.ShapeDtypeStruct((n, D_MODEL), F32),
        compiler_params=_cparams(1),
        name="ln2",
    )(x1, yk, tg, g, b)


def _rot_cols(w, n_heads):
    lead = w.shape[:-1]
    w4 = w.reshape(*lead, n_heads, 2, HEAD_DIM // 2)
    return jnp.concatenate([-w4[..., 1, :], w4[..., 0, :]], axis=-1).reshape(*lead, n_heads * HEAD_DIM)


def _pair_major(w):
    lead = w.shape[:-1]
    w4 = w.reshape(*lead, KV_HEADS, 2, HEAD_DIM)
    return jnp.swapaxes(w4, -3, -2).reshape(*lead, ATTN_WIDTH)


def _prep_weights(w_in, w_pa, w_router, b_router):
    c = np.cumsum((ATTN_WIDTH, KV_WIDTH, KV_WIDTH, IDX_WIDTH, IDX_DIM, IDX_HEADS,
                   CONV_DIM, CONV_DIM, CONV_DIM, D_MODEL, D_MODEL)).tolist()
    wq, wk, wv = w_in[..., :c[0]], w_in[..., c[0]:c[1]], w_in[..., c[1]:c[2]]
    wqi, wki, wwi = w_in[..., c[2]:c[3]], w_in[..., c[3]:c[4]], w_in[..., c[4]:c[5]]
    wconv = w_in[..., c[5]:c[8]]
    wgate = w_in[..., c[8]:]
    wki_r = _rot_cols(wki, 1)
    wwi = jnp.pad(wwi * (IDX_HEADS ** -0.5 * IDX_DIM ** -0.5), ((0, 0), (0, 0), (0, LANES - IDX_HEADS)))
    widx = jnp.concatenate([wqi, _rot_cols(wqi, IDX_HEADS), jnp.tile(wki, (1, 1, IDX_HEADS)),
                            jnp.tile(wki_r, (1, 1, IDX_HEADS)), wwi], axis=-1)
    widx_hi, w1lo = _split_hi_lo(widx)
    w1cat = jnp.concatenate(
        [_pair_major(wq).astype(BF16), _pair_major(_rot_cols(wq, N_HEADS)).astype(BF16), wk.astype(BF16),
         _rot_cols(wk, KV_HEADS).astype(BF16), wv.astype(BF16), widx_hi, wconv.astype(BF16)], axis=-1)
    wpa = jnp.swapaxes(w_pa.reshape(DEPTH, KV_HEADS, 2, HEAD_DIM, D_MODEL), 1, 2).reshape(DEPTH, ATTN_WIDTH, D_MODEL)
    wr, wrl = _split_hi_lo(jnp.pad(w_router, ((0, 0), (0, 0), (0, LANES - N_EXPERTS))))
    br = jnp.pad(b_router, ((0, 0), (0, LANES - N_EXPERTS)), constant_values=-jnp.inf)[:, None, :]
    return w1cat, w1lo, wgate.astype(BF16), wpa.astype(BF16), wr, wrl, br


def _rope_tables():
    pos = jnp.concatenate([jnp.tile(jnp.arange(SEQ, dtype=I32), BATCH),
                           jnp.tile(PAST_LEN + jnp.arange(DEC_SEQ, dtype=I32), DEC_BATCH)])
    half = HEAD_DIM // 2
    inv = ROPE_THETA ** (-jnp.arange(half, dtype=F32) / half)
    ang = pos.astype(F32)[:, None] * inv[None, :]
    return jnp.tile(jnp.cos(ang), (1, LANES // half)), jnp.tile(jnp.sin(ang), (1, LANES // half))


def kernel(x_prompt, x_sample, cache_k, cache_v, cache_kidx, state_conv, page_table, w_in, w_conv, w_pa, w_pb,
           w_o, ln1_g, ln1_b, w_router, b_router, w1, b1, w2, b2, ln2_g, ln2_b):
    w1cat, w1lo, wgate, wpa, wr, wrl, br = _prep_weights(w_in, w_pa, w_router, b_router)
    wpb = w_pb.astype(BF16)
    wo = w_o.astype(BF16)
    wc = jnp.pad(w_conv, ((0, 0), (0, 8 - w_conv.shape[1]), (0, 0)))
    b1g = b1[..., 0::2][:, :, None, :]
    b1l = b1[..., 1::2][:, :, None, :]
    b2r = b2[:, :, None, :]
    cos, sin = _rope_tables()
    ckT = jnp.transpose(cache_k, (0, 1, 3, 4, 2)).reshape(DEPTH, -1, KV_WIDTH, PAGE_SIZE)
    cvT = jnp.transpose(cache_v, (0, 1, 3, 4, 2)).reshape(DEPTH, -1, KV_WIDTH, PAGE_SIZE)
    ckiT = jnp.transpose(cache_kidx, (0, 1, 3, 2))
    pt = page_table.reshape(-1).astype(I32)

    x = jnp.concatenate([x_prompt.reshape(N_P, D_MODEL), x_sample.reshape(N_S, D_MODEL)], axis=0)
    ks, vs, kis, convs = [], [], [], []
    for l in range(DEPTH):
        q, qi, wi, k, v, ki4, kb, vb, kib, kil, cb, u = _inproj(x, w1cat[l], w1lo[l], cos, sin)
        a_p = _attn_prompt(q, qi, wi, kb, vb, kib, kil)
        a_s = _attn_sample(l, pt, q, qi, wi, k, v, ki4, ckT, cvT, ckiT)
        a = jnp.concatenate([a_p, a_s], axis=0)

        up = u[:N_P].reshape(BATCH, SEQ, CONV_DIM)
        us = jnp.concatenate([state_conv[l], u[N_P:].reshape(DEC_BATCH, DEC_SEQ, CONV_DIM)], axis=1)
        zp = jnp.zeros((BATCH, 2, CONV_DIM), F32)
        um1 = jnp.concatenate([jnp.concatenate([zp[:, :1], up[:, :-1]], axis=1).reshape(N_P, CONV_DIM),
                               us[:, 1:1 + DEC_SEQ].reshape(N_S, CONV_DIM)], axis=0)
        um2 = jnp.concatenate([jnp.concatenate([zp, up[:, :-2]], axis=1).reshape(N_P, CONV_DIM),
                               us[:, 0:DEC_SEQ].reshape(N_S, CONV_DIM)], axis=0)

        x1, te, tg = _merge(x, a, cb, u, um1, um2, wgate[l], wpa[l], wpb[l], wo[l], wc[l],
                            ln1_g[l][None], ln1_b[l][None], wr[l], wrl[l], br[l])
        dest, row_tok, blk_e, n_act = _route(te[:, :TOP_K])
        xs = _sc_gather(x1, row_tok)
        y_rows = _moe(blk_e, n_act, xs, w1[l], b1g[l], b1l[l], w2[l], b2r[l])
        yk = _sc_gather(y_rows, dest.reshape(N_TOK, TOP_K).T.reshape(-1)).reshape(TOP_K, N_TOK, D_MODEL)
        x = _ln2(x1, yk, tg, ln2_g[l][None], ln2_b[l][None])

        ks.append(k)
        vs.append(v)
        kis.append(ki4[:, :IDX_DIM])
        convs.append((up[:, -2:], us[:, -2:]))

    def split(parts, tail):
        st = jnp.stack(parts, axis=0)
        return (st[:, :N_P].reshape(DEPTH, BATCH, SEQ, *tail), st[:, N_P:].reshape(DEPTH, DEC_BATCH, DEC_SEQ, *tail))

    k_p, k_s = split(ks, (KV_HEADS, HEAD_DIM))
    v_p, v_s = split(vs, (KV_HEADS, HEAD_DIM))
    ki_p, ki_s = split(kis, (IDX_DIM,))
    conv_p = jnp.stack([c_[0] for c_ in convs], axis=0)
    conv_s = jnp.stack([c_[1] for c_ in convs], axis=0)
    y_p = x[:N_P].reshape(BATCH, SEQ, D_MODEL)
    y_s = x[N_P:].reshape(DEC_BATCH, DEC_SEQ, D_MODEL)
    return (y_p, y_s, k_p, v_p, ki_p, conv_p, k_s, v_s, ki_s, conv_s)
```

```python
import functools

import numpy as np
import jax
import jax.numpy as jnp
from jax import lax
from jax.experimental import pallas as pl
from jax.experimental.pallas import tpu as pltpu
from jax.experimental.pallas import tpu_sc as plsc

F32 = jnp.float32
BF16 = jnp.bfloat16
I32 = jnp.int32

D_MODEL = 1024
BATCH = 4
SEQ = 4096
DEPTH = 4
DEC_BATCH = 128
DEC_SEQ = 8
PAGE_SIZE = 128
N_PAGES = 16
PAST_LEN = N_PAGES * PAGE_SIZE
N_HEADS = 8
HEAD_DIM = 64
KV_HEADS = 4
IDX_HEADS = 4
IDX_DIM = 64
TOPK = 256
CONV_DIM = 512
N_EXPERTS = 32
TOP_K = 4
D_EXPERT = 1024
SWIGLU_LIMIT = 7.0
SWIGLU_ALPHA = 1.702
ROPE_THETA = 10000.0
LN_EPS = 1e-5
ATTN_WIDTH = N_HEADS * HEAD_DIM
KV_WIDTH = KV_HEADS * HEAD_DIM
IDX_WIDTH = IDX_HEADS * IDX_DIM
DEEPNORM_ALPHA = (2 * DEPTH) ** 0.25

N_P = BATCH * SEQ
N_S = DEC_BATCH * DEC_SEQ
N_TOK = N_P + N_S

LANES = 128
VMEM_LIMIT = 56 * 1024 * 1024
INT_MIN = -2 ** 31

_SEC = {}
_off = 0
for _name, _w in (("q", 512), ("qr", 512), ("k", 256), ("kr", 256), ("v", 256), ("qi", 256), ("qir", 256),
                  ("ki", 256), ("kir", 256), ("wi", 128), ("cb", 512), ("cc", 512), ("cx", 512)):
    _SEC[_name] = (_off, _off + _w)
    _off += _w
W1_COLS = _off
_SEC_LO = {}
_off = 0
for _name in ("qi", "qir", "ki", "kir", "wi"):
    _w = _SEC[_name][1] - _SEC[_name][0]
    _SEC_LO[_name] = (_off, _off + _w)
    _off += _w
W1_LO_COLS = _off

TM = 512
TQ = 256
MOE_BM = 256
MOE_ROWS = N_TOK * TOP_K + N_EXPERTS * MOE_BM
MOE_NBLK = MOE_ROWS // MOE_BM


def _cparams(n_axes):
    return pltpu.CompilerParams(dimension_semantics=("arbitrary",) * n_axes, vmem_limit_bytes=VMEM_LIMIT)


def _const_spec(shape):
    nd = len(shape)
    return pl.BlockSpec(shape, lambda *_: (0,) * nd)


def _split_hi_lo(x):
    hi = lax.bitcast_convert_type(lax.bitcast_convert_type(x, I32) & jnp.int32(-65536), F32)
    return hi.astype(BF16), (x - hi).astype(BF16)


def _inproj_kernel(x_ref, w_ref, wlo_ref, cos_ref, sin_ref,
                   q_ref, qi_ref, wi_ref, k_ref, v_ref, ki_ref, kb_ref, vb_ref, kix_ref, cb_ref, u_ref):
    x = x_ref[...]
    xb = x.astype(BF16)
    xhi, xlo = _split_hi_lo(x)

    def mm(name):
        a, b = _SEC[name]
        if name not in _SEC_LO:
            return jnp.dot(xb, w_ref[:, a:b], preferred_element_type=F32)
        la, lb = _SEC_LO[name]
        return (jnp.dot(xhi, w_ref[:, a:b], preferred_element_type=F32)
                + jnp.dot(xlo, w_ref[:, a:b], preferred_element_type=F32)
                + jnp.dot(xhi, wlo_ref[:, la:lb], preferred_element_type=F32)
                + jnp.dot(xlo, wlo_ref[:, la:lb], preferred_element_type=F32))

    cos = cos_ref[...]
    sin = sin_ref[...]
    cos2 = jnp.concatenate([cos, cos], axis=1)
    sin2 = jnp.concatenate([sin, sin], axis=1)
    cos4 = jnp.concatenate([cos2, cos2], axis=1)
    sin4 = jnp.concatenate([sin2, sin2], axis=1)
    q_ref[...] = mm("q") * cos4 + mm("qr") * sin4
    qi_ref[...] = mm("qi") * cos2 + mm("qir") * sin2
    k = mm("k") * cos2 + mm("kr") * sin2
    k_ref[...] = k
    kb_ref[...] = k.astype(BF16)
    ki = mm("ki") * cos2 + mm("kir") * sin2
    ki_ref[...] = ki
    ki_hi, ki_lo = _split_hi_lo(ki)
    kix_ref[...] = jnp.concatenate([ki_hi[:, :LANES], ki_lo[:, :LANES]], axis=1)
    v = mm("v")
    v_ref[...] = v
    vb_ref[...] = v.astype(BF16)
    wi_ref[...] = mm("wi")
    cb_ref[...] = mm("cb")
    u_ref[...] = mm("cc") * mm("cx")


def _inproj(x, w, wlo, cos, sin):
    n = x.shape[0]
    row = lambda w_: pl.BlockSpec((TM, w_), lambda i: (i, 0))
    widths = (512, 256, 128, 256, 256, 256, 256, 256, 256, 512, 512)
    dtypes = (F32, F32, F32, F32, F32, F32, BF16, BF16, BF16, F32, F32)
    return pl.pallas_call(
        _inproj_kernel,
        grid=(n // TM,),
        in_specs=[row(D_MODEL), _const_spec((D_MODEL, W1_COLS)), _const_spec((D_MODEL, W1_LO_COLS)),
                  row(LANES), row(LANES)],
        out_specs=[row(w_) for w_ in widths],
        out_shape=[jax.ShapeDtypeStruct((n, w_), d_) for w_, d_ in zip(widths, dtypes)],
        compiler_params=_cparams(1),
        name="inproj",
    )(x, w, wlo, cos, sin)


def _sort_key(score, valid):
    bits = pltpu.bitcast(score, I32)
    key = jnp.where(bits < 0, bits ^ jnp.int32(0x7FFFFFFF), bits)
    key = jnp.where(score == 0.0, jnp.int32(0), key)
    return jnp.where(valid, key, jnp.int32(INT_MIN))


def _count(mask):
    return jnp.sum(jnp.where(mask, 1.0, 0.0), axis=1, keepdims=True)


def _select_topk(key_ref, c0_ref, s_idx, n_keys, two_bits=False):
    rows = key_ref.shape[0]
    kf = float(TOPK)

    def thr_step(it, t):
        cand = t + jnp.left_shift(jnp.int32(1), 31 - it)
        cnt = _count(key_ref[...] >= cand)
        return jnp.where(cnt >= kf, cand, t)

    def thr_step2(it, t):
        d = jnp.left_shift(jnp.int32(1), 30 - 2 * it)
        c1 = t + d
        c2 = c1 + d
        c3 = c2 + d
        key = key_ref[...]
        n1, n2, n3 = _count(key >= c1), _count(key >= c2), _count(key >= c3)
        return jnp.where(n3 >= kf, c3, jnp.where(n2 >= kf, c2, jnp.where(n1 >= kf, c1, t)))

    thr0 = jnp.full((rows, 1), INT_MIN, I32)
    thr = lax.fori_loop(0, 16, thr_step2, thr0) if two_bits else lax.fori_loop(0, 32, thr_step, thr0)
    key = key_ref[...]
    live = key > jnp.int32(INT_MIN)
    n_gt = _count(key > thr)
    n_ge = _count(jnp.logical_and(key >= thr, live))
    c0_ref[...] = jnp.full(c0_ref.shape, n_keys, I32)

    @pl.when(jnp.max(n_ge) > kf)
    def _():
        need = kf - n_gt
        nbits = int(n_keys).bit_length()

        def idx_step(it, c):
            cand = c + jnp.left_shift(jnp.int32(1), nbits - 1 - it)
            eq = jnp.logical_and(key_ref[...] == thr, s_idx < cand)
            return jnp.where(_count(eq) < need, cand, c)

        c_last = lax.fori_loop(0, nbits, idx_step, jnp.zeros((rows, 1), I32))
        c0_ref[...] = jnp.broadcast_to(c_last, c0_ref.shape)

    c0 = c0_ref[:, 0:1]
    tie = jnp.logical_and(key == thr, s_idx <= c0)
    return jnp.logical_and(jnp.logical_or(key > thr, tie), live)


def _head_mask(x, g):
    lane = lax.broadcasted_iota(I32, x.shape, 1)
    return jnp.where(jnp.right_shift(lane, 6) == g, x, 0.0)


_NT = (((1,), (1,)), ((), ()))


def _attn_p_kernel(q_ref, qi_ref, wi_ref, k_ref, v_ref, kix_ref, o_ref, key_ref, neg_ref, c0_ref, *, n_keys, q_base):
    t0 = q_base + pl.program_id(1) * TQ
    qi = qi_ref[...]
    wi = wi_ref[...]
    kix = kix_ref[...]
    lane_blk = jnp.right_shift(lax.broadcasted_iota(I32, (TQ, IDX_WIDTH), 1), 6)
    score = jnp.zeros((TQ, n_keys), F32)
    for h in range(IDX_HEADS):
        x = _head_mask(qi, h)
        x = x + pltpu.roll(x, 2 * IDX_DIM, axis=1)
        x = x + pltpu.roll(x, IDX_DIM, axis=1)
        x_hi, x_lo = _split_hi_lo(x)
        qx = jnp.where((lane_blk & 1) == 0, x_hi, x_lo)
        rel = lax.dot_general(qx, kix, _NT, preferred_element_type=F32)
        score = score + wi[:, h:h + 1] * jnp.maximum(rel, 0.0)
    s_idx = lax.broadcasted_iota(I32, (TQ, n_keys), 1)
    t_idx = t0 + lax.broadcasted_iota(I32, (TQ, n_keys), 0)
    key_ref[...] = _sort_key(score, s_idx <= t_idx)
    sel = _select_topk(key_ref, c0_ref, s_idx, n_keys)
    neg_ref[...] = jnp.where(sel, 0.0, -jnp.inf)

    kb = k_ref[...]
    vb = v_ref[...]
    for hq in range(2):
        qh = q_ref[:, hq * KV_WIDTH:(hq + 1) * KV_WIDTH]
        acc = jnp.zeros((TQ, KV_WIDTH), F32)
        for g in range(KV_HEADS):
            qg = (_head_mask(qh, g) * HEAD_DIM ** -0.5).astype(BF16)
            lg = lax.dot_general(qg, kb, _NT, preferred_element_type=F32) + neg_ref[...]
            p = jnp.exp(lg - jnp.max(lg, axis=1, keepdims=True))
            inv = 1.0 / jnp.sum(p, axis=1, keepdims=True)
            pv = jnp.dot(p.astype(BF16), vb, preferred_element_type=F32)
            acc = acc + _head_mask(pv * inv, g)
        o_ref[:, hq * KV_WIDTH:(hq + 1) * KV_WIDTH] = acc


def _attn_p_group(q, qi, wi, kb, vb, kix, q_lo, q_hi, n_keys):
    nq = (q_hi - q_lo) // TQ
    qrow = lambda w_: pl.BlockSpec((TQ, w_), lambda b, i: ((b * SEQ + q_lo) // TQ + i, 0))
    krow = pl.BlockSpec((None, n_keys, KV_WIDTH), lambda b, i: (b, 0, 0))
    out = pl.pallas_call(
        functools.partial(_attn_p_kernel, n_keys=n_keys, q_base=q_lo),
        grid=(BATCH, nq),
        in_specs=[qrow(ATTN_WIDTH), qrow(IDX_WIDTH), qrow(LANES), krow, krow, krow],
        out_specs=pl.BlockSpec((TQ, ATTN_WIDTH), lambda b, i: (b * nq + i, 0)),
        out_shape=jax.ShapeDtypeStruct((BATCH * (q_hi - q_lo), ATTN_WIDTH), F32),
        scratch_shapes=[pltpu.VMEM((TQ, n_keys), I32), pltpu.VMEM((TQ, n_keys), F32), pltpu.VMEM((TQ, LANES), I32)],
        compiler_params=_cparams(2),
        name=f"attn_p{n_keys}",
    )(q, qi, wi, kb, vb, kix)
    return out.reshape(BATCH, q_hi - q_lo, ATTN_WIDTH)


P_GROUP = 512
_P_GROUPS = tuple((lo, lo + P_GROUP, lo + P_GROUP) for lo in range(0, SEQ, P_GROUP))


def _attn_prompt(q, qi, wi, kb, vb, kix):
    seq3 = lambda a_: a_[:N_P].reshape(BATCH, SEQ, KV_WIDTH)
    kb, vb, kix = seq3(kb), seq3(vb), seq3(kix)
    parts = [_attn_p_group(q, qi, wi, kb, vb, kix, lo, hi, nk) for lo, hi, nk in _P_GROUPS]
    return jnp.concatenate(parts, axis=1).reshape(N_P, ATTN_WIDTH)


S_KEYS = PAST_LEN + LANES


def _attn_s_kernel(pt_ref, q_ref, qi_ref, wi_ref, kn_ref, vn_ref, kin_ref, *rest):
    kT = rest[0:N_PAGES]
    vT = rest[N_PAGES:2 * N_PAGES]
    kiT = rest[2 * N_PAGES:3 * N_PAGES]
    o_ref, kTs, vTs, kiTs, key_ref, c0_ref = rest[3 * N_PAGES:]
    del pt_ref
    for j in range(N_PAGES):
        sl = slice(j * PAGE_SIZE, (j + 1) * PAGE_SIZE)
        kTs[:, sl] = kT[j][...].astype(BF16)
        vTs[:, sl] = vT[j][...].astype(BF16)
        kiTs[:, sl] = kiT[j][...].astype(BF16)

    pad = jnp.zeros((LANES - DEC_SEQ, KV_WIDTH), F32)
    kn = jnp.concatenate([kn_ref[...], pad], axis=0).astype(BF16)
    vn = jnp.concatenate([vn_ref[...], pad], axis=0).astype(BF16)
    kin = jnp.concatenate([kin_ref[...], pad], axis=0)[:, :IDX_DIM].astype(BF16)

    qi = qi_ref[...]
    wi = wi_ref[...]
    qis = jnp.concatenate([qi[:, h * IDX_DIM:(h + 1) * IDX_DIM] for h in range(IDX_HEADS)], axis=0).astype(BF16)
    rel = jnp.concatenate([jnp.dot(qis, kiTs[...], preferred_element_type=F32),
                           lax.dot_general(qis, kin, _NT, preferred_element_type=F32)], axis=1)
    rel = jnp.maximum(rel, 0.0)
    score = jnp.zeros((DEC_SEQ, S_KEYS), F32)
    for h in range(IDX_HEADS):
        score = score + wi[:, h:h + 1] * rel[h * DEC_SEQ:(h + 1) * DEC_SEQ]
    s_idx = lax.broadcasted_iota(I32, (DEC_SEQ, S_KEYS), 1)
    t_idx = PAST_LEN + lax.broadcasted_iota(I32, (DEC_SEQ, S_KEYS), 0)
    key_ref[...] = _sort_key(score, s_idx <= t_idx)
    sel = _select_topk(key_ref, c0_ref, s_idx, S_KEYS, two_bits=True)
    neg = jnp.where(sel, 0.0, -jnp.inf)

    q = q_ref[...]
    qs = jnp.concatenate([_head_mask(q[:, hq * KV_WIDTH:(hq + 1) * KV_WIDTH], g)
                          for hq in range(2) for g in range(KV_HEADS)], axis=0).astype(BF16)
    lg = jnp.concatenate([jnp.dot(qs, kTs[...], preferred_element_type=F32),
                          lax.dot_general(qs, kn, _NT, preferred_element_type=F32)], axis=1)
    lg = lg * HEAD_DIM ** -0.5 + jnp.concatenate([neg] * N_HEADS, axis=0)
    p = jnp.exp(lg - jnp.max(lg, axis=1, keepdims=True))
    inv = 1.0 / jnp.sum(p, axis=1, keepdims=True)
    pb = p.astype(BF16)
    pv = (lax.dot_general(pb[:, :PAST_LEN], vTs[...], _NT, preferred_element_type=F32)
          + jnp.dot(pb[:, PAST_LEN:], vn, preferred_element_type=F32)) * inv
    for hq in range(2):
        acc = jnp.zeros((DEC_SEQ, KV_WIDTH), F32)
        for g in range(KV_HEADS):
            r = (hq * KV_HEADS + g) * DEC_SEQ
            acc = acc + _head_mask(pv[r:r + DEC_SEQ], g)
        o_ref[:, hq * KV_WIDTH:(hq + 1) * KV_WIDTH] = acc


def _attn_sample(layer, page_table, q, qi, wi, k, v, ki, ckT, cvT, ckiT):
    srow = lambda w_: pl.BlockSpec((DEC_SEQ, w_), lambda b, pt: (N_P // DEC_SEQ + b, 0))

    def page(rows, j):
        return pl.BlockSpec((None, None, rows, PAGE_SIZE), lambda b, pt: (layer, pt[b * N_PAGES + j], 0, 0))

    in_specs = [srow(ATTN_WIDTH), srow(IDX_WIDTH), srow(LANES), srow(KV_WIDTH), srow(KV_WIDTH), srow(KV_WIDTH)]
    in_specs += [page(KV_WIDTH, j) for j in range(N_PAGES)]
    in_specs += [page(KV_WIDTH, j) for j in range(N_PAGES)]
    in_specs += [page(IDX_DIM, j) for j in range(N_PAGES)]
    grid_spec = pltpu.PrefetchScalarGridSpec(
        num_scalar_prefetch=1,
        grid=(DEC_BATCH,),
        in_specs=in_specs,
        out_specs=pl.BlockSpec((DEC_SEQ, ATTN_WIDTH), lambda b, pt: (b, 0)),
        scratch_shapes=[pltpu.VMEM((KV_WIDTH, PAST_LEN), BF16), pltpu.VMEM((KV_WIDTH, PAST_LEN), BF16),
                        pltpu.VMEM((IDX_DIM, PAST_LEN), BF16), pltpu.VMEM((DEC_SEQ, S_KEYS), I32),
                        pltpu.VMEM((DEC_SEQ, LANES), I32)],
    )
    return pl.pallas_call(
        _attn_s_kernel,
        grid_spec=grid_spec,
        out_shape=jax.ShapeDtypeStruct((N_S, ATTN_WIDTH), F32),
        compiler_params=_cparams(1),
        name="attn_s",
    )(page_table, q, qi, wi, k, v, ki, *([ckT] * N_PAGES), *([cvT] * N_PAGES), *([ckiT] * N_PAGES))


def _layer_norm(y, g, b):
    mu = jnp.mean(y, axis=1, keepdims=True)
    d = y - mu
    var = jnp.mean(d * d, axis=1, keepdims=True)
    return d * lax.rsqrt(var + LN_EPS) * g + b


def _merge_kernel(x_ref, a_ref, cb_ref, u_ref, um1_ref, um2_ref, wg_ref, wpa_ref, wpb_ref, wo_ref, wc_ref,
                  g_ref, b_ref, wr_ref, wrl_ref, br_ref, x1_ref, te_ref, tg_ref):
    x = x_ref[...]
    gates = jax.nn.sigmoid(jnp.dot(x.astype(BF16), wg_ref[...], preferred_element_type=F32))
    wc = wc_ref[...]
    conv = wc[0:1] * um2_ref[...] + wc[1:2] * um1_ref[...] + wc[2:3] * u_ref[...]
    yc = cb_ref[...] * conv
    pa = jnp.dot(a_ref[...].astype(BF16), wpa_ref[...], preferred_element_type=F32)
    pb = jnp.dot(yc.astype(BF16), wpb_ref[...], preferred_element_type=F32)
    merged = gates[:, :D_MODEL] * pa + gates[:, D_MODEL:] * pb
    mix = jnp.dot(merged.astype(BF16), wo_ref[...], preferred_element_type=F32)
    x1 = _layer_norm(DEEPNORM_ALPHA * x + mix, g_ref[...], b_ref[...])
    x1_ref[...] = x1

    x1h, x1l = _split_hi_lo(x1)
    wrh = wr_ref[...]
    wrl = wrl_ref[...]
    lg = (jnp.dot(x1h, wrh, preferred_element_type=F32) + jnp.dot(x1l, wrh, preferred_element_type=F32)
          + jnp.dot(x1h, wrl, preferred_element_type=F32) + jnp.dot(x1l, wrl, preferred_element_type=F32)
          + br_ref[...])
    lane = lax.broadcasted_iota(I32, lg.shape, 1)
    te = jnp.zeros(lg.shape, I32)
    vals = []
    for r in range(TOP_K):
        m = jnp.max(lg, axis=1, keepdims=True)
        idx = jnp.min(jnp.where(lg == m, lane, LANES), axis=1, keepdims=True)
        te = jnp.where(lane == r, idx, te)
        vals.append(m)
        lg = jnp.where(lane == idx, -jnp.inf, lg)
    ex = [jnp.exp(v_ - vals[0]) for v_ in vals]
    inv = 1.0 / (ex[0] + ex[1] + ex[2] + ex[3])
    tg = jnp.zeros(lg.shape, F32)
    for r in range(TOP_K):
        tg = jnp.where(lane == r, ex[r] * inv, tg)
    te_ref[...] = te
    tg_ref[...] = tg


def _merge(x, a, cb, u, um1, um2, wg, wpa, wpb, wo, wc, g, b, wr, wrl, br):
    n = x.shape[0]
    row = lambda w_: pl.BlockSpec((TM, w_), lambda i: (i, 0))
    return pl.pallas_call(
        _merge_kernel,
        grid=(n // TM,),
        in_specs=[row(D_MODEL), row(ATTN_WIDTH), row(CONV_DIM), row(CONV_DIM), row(CONV_DIM), row(CONV_DIM),
                  _const_spec(wg.shape), _const_spec(wpa.shape), _const_spec(wpb.shape), _const_spec(wo.shape),
                  _const_spec(wc.shape), _const_spec(g.shape), _const_spec(b.shape),
                  _const_spec(wr.shape), _const_spec(wrl.shape), _const_spec(br.shape)],
        out_specs=[row(D_MODEL), row(LANES), row(LANES)],
        out_shape=[jax.ShapeDtypeStruct((n, D_MODEL), F32),
                   jax.ShapeDtypeStruct((n, LANES), I32), jax.ShapeDtypeStruct((n, LANES), F32)],
        compiler_params=_cparams(1),
        name="merge",
    )(x, a, cb, u, um1, um2, wg, wpa, wpb, wo, wc, g, b, wr, wrl, br)


PAIR_W = 2 * LANES


def _moe_kernel(be_ref, na_ref, xs_ref, w1_ref, b1g_ref, b1l_ref, w2_ref, b2_ref, sel_ref, o_ref,
                w1g_s, w1l_s, w2_s):
    i = pl.program_id(0)
    active = i < na_ref[0]
    new_expert = jnp.logical_or(i == 0, be_ref[i] != be_ref[jnp.maximum(i - 1, 0)])

    @pl.when(jnp.logical_and(active, new_expert))
    def _():
        sel = sel_ref[...]
        for c in range(2 * D_EXPERT // PAIR_W):
            wc = w1_ref[:, c * PAIR_W:(c + 1) * PAIR_W].astype(BF16)
            d = jnp.dot(wc, sel, preferred_element_type=F32).astype(BF16)
            w1g_s[:, c * LANES:(c + 1) * LANES] = d[:, :LANES]
            w1l_s[:, c * LANES:(c + 1) * LANES] = d[:, LANES:]
        w2_s[...] = w2_ref[...].astype(BF16)

    @pl.when(active)
    def _():
        xs = xs_ref[...].astype(BF16)
        hg = jnp.dot(xs, w1g_s[...], preferred_element_type=F32) + b1g_ref[...]
        hl = jnp.dot(xs, w1l_s[...], preferred_element_type=F32) + b1l_ref[...]
        glu = jnp.minimum(hg, SWIGLU_LIMIT)
        lin = jnp.clip(hl, -SWIGLU_LIMIT, SWIGLU_LIMIT)
        act = glu * jax.nn.sigmoid(SWIGLU_ALPHA * glu) * (lin + 1.0)
        o_ref[...] = jnp.dot(act.astype(BF16), w2_s[...], preferred_element_type=F32) + b2_ref[...]

    @pl.when(jnp.logical_not(active))
    def _():
        o_ref[...] = jnp.zeros(o_ref.shape, F32)


def _pair_select():
    r = np.arange(PAIR_W)[:, None]
    c = np.arange(PAIR_W)[None, :]
    src = np.where(c < LANES, 2 * c, 2 * (c - LANES) + 1)
    return jnp.asarray(r == src, BF16)


def _moe(layer, blk_e, n_act, xs, w1, b1g, b1l, w2, b2):
    wspec = lambda r, c: pl.BlockSpec((None, None, r, c), lambda i, be, na: (layer, be[i], 0, 0))
    grid_spec = pltpu.PrefetchScalarGridSpec(
        num_scalar_prefetch=2,
        grid=(MOE_NBLK,),
        in_specs=[pl.BlockSpec((MOE_BM, D_MODEL), lambda i, be, na: (i, 0)),
                  wspec(D_MODEL, 2 * D_EXPERT), wspec(1, D_EXPERT), wspec(1, D_EXPERT),
                  wspec(D_EXPERT, D_MODEL), wspec(1, D_MODEL),
                  pl.BlockSpec((PAIR_W, PAIR_W), lambda i, be, na: (0, 0))],
        out_specs=pl.BlockSpec((MOE_BM, D_MODEL), lambda i, be, na: (i, 0)),
        scratch_shapes=[pltpu.VMEM((D_MODEL, D_EXPERT), BF16), pltpu.VMEM((D_MODEL, D_EXPERT), BF16),
                        pltpu.VMEM((D_EXPERT, D_MODEL), BF16)],
    )
    return pl.pallas_call(
        _moe_kernel,
        grid_spec=grid_spec,
        out_shape=jax.ShapeDtypeStruct((MOE_ROWS, D_MODEL), F32),
        compiler_params=_cparams(1),
        name="moe",
    )(blk_e, n_act, xs, w1, b1g, b1l, w2, b2, _pair_select())


SC_ROWS = 64


def _sc_gather(table, idx):
    n_idx = idx.shape[0]
    width = table.shape[1]
    info = plsc.get_sparse_core_info()
    n_cores = info.num_cores
    n_workers = n_cores * info.num_subcores
    per_worker = n_idx // n_workers
    assert per_worker * n_workers == n_idx and per_worker % SC_ROWS == 0
    mesh = plsc.VectorSubcoreMesh(core_axis_name="c", subcore_axis_name="s")

    @functools.partial(
        pl.kernel, mesh=mesh,
        out_type=jax.ShapeDtypeStruct((n_idx, width), table.dtype),
        scratch_types=[pltpu.VMEM((SC_ROWS,), I32), pltpu.VMEM((SC_ROWS, width), table.dtype),
                       pltpu.SemaphoreType.DMA],
    )
    def gather_kernel(table_hbm, idx_hbm, out_hbm, idx_v, rows_v, sem):
        base = (lax.axis_index("s") * n_cores + lax.axis_index("c")) * per_worker

        @pl.loop(0, per_worker // SC_ROWS)
        def _(j):
            off = base + j * SC_ROWS
            pltpu.sync_copy(idx_hbm.at[pl.ds(off, SC_ROWS)], idx_v)
            pltpu.async_copy(table_hbm.at[idx_v], rows_v, sem).wait()
            pltpu.sync_copy(rows_v, out_hbm.at[pl.ds(off, SC_ROWS)])

    return gather_kernel(table, idx)


def _route(top_e):
    flat_e = top_e.reshape(-1)
    onehot = (flat_e[:, None] == jnp.arange(N_EXPERTS, dtype=I32)[None, :]).astype(I32)
    csum = jnp.cumsum(onehot, axis=0)
    counts = csum[-1]
    rank = jnp.take_along_axis(csum, flat_e[:, None], axis=1)[:, 0] - 1
    padded = (counts + MOE_BM - 1) // MOE_BM * MOE_BM
    pad_end = jnp.cumsum(padded)
    pad_start = pad_end - padded
    dest = pad_start[flat_e] + rank
    flat_tok = jnp.repeat(jnp.arange(N_TOK, dtype=I32), TOP_K)
    row_tok = (jnp.arange(MOE_ROWS, dtype=I32) % N_TOK).at[dest].set(flat_tok)
    blk_start = jnp.arange(MOE_NBLK, dtype=I32) * MOE_BM
    blk_e = jnp.minimum(jnp.sum(blk_start[:, None] >= pad_end[None, :], axis=1), N_EXPERTS - 1).astype(I32)
    n_act = (pad_end[-1] // MOE_BM).astype(I32).reshape(1)
    return dest, row_tok, blk_e, n_act


def _ln2_kernel(x_ref, y_ref, tg_ref, g_ref, b_ref, o_ref):
    tg = tg_ref[...]
    y = tg[:, 0:1] * y_ref[0]
    for r in range(1, TOP_K):
        y = y + tg[:, r:r + 1] * y_ref[r]
    o_ref[...] = _layer_norm(DEEPNORM_ALPHA * x_ref[...] + y, g_ref[...], b_ref[...])


def _ln2(x1, yk, tg, g, b):
    n = x1.shape[0]
    return pl.pallas_call(
        _ln2_kernel,
        grid=(n // TM,),
        in_specs=[pl.BlockSpec((TM, D_MODEL), lambda i: (i, 0)),
                  pl.BlockSpec((TOP_K, TM, D_MODEL), lambda i: (0, i, 0)),
                  pl.BlockSpec((TM, LANES), lambda i: (i, 0)),
                  _const_spec(g.shape), _const_spec(b.shape)],
        out_specs=pl.BlockSpec((TM, D_MODEL), lambda i: (i, 0)),
        out_shape=jax.ShapeDtypeStruct((n, D_MODEL), F32),
        compiler_params=_cparams(1),
        name="ln2",
    )(x1, yk, tg, g, b)


def _rot_cols(w, n_heads):
    lead = w.shape[:-1]
    w4 = w.reshape(*lead, n_heads, 2, HEAD_DIM // 2)
    return jnp.concatenate([-w4[..., 1, :], w4[..., 0, :]], axis=-1).reshape(*lead, n_heads * HEAD_DIM)


def _pair_major(w):
    lead = w.shape[:-1]
    w4 = w.reshape(*lead, KV_HEADS, 2, HEAD_DIM)
    return jnp.swapaxes(w4, -3, -2).reshape(*lead, ATTN_WIDTH)


def _prep_weights(w_in, w_pa, w_router, b_router):
    c = np.cumsum((ATTN_WIDTH, KV_WIDTH, KV_WIDTH, IDX_WIDTH, IDX_DIM, IDX_HEADS,
                   CONV_DIM, CONV_DIM, CONV_DIM, D_MODEL, D_MODEL)).tolist()
    wq, wk, wv = w_in[..., :c[0]], w_in[..., c[0]:c[1]], w_in[..., c[1]:c[2]]
    wqi, wki, wwi = w_in[..., c[2]:c[3]], w_in[..., c[3]:c[4]], w_in[..., c[4]:c[5]]
    wconv = w_in[..., c[5]:c[8]]
    wgate = w_in[..., c[8]:]
    wki_r = _rot_cols(wki, 1)
    wwi = jnp.pad(wwi * (IDX_HEADS ** -0.5 * IDX_DIM ** -0.5), ((0, 0), (0, 0), (0, LANES - IDX_HEADS)))
    widx = jnp.concatenate([wqi, _rot_cols(wqi, IDX_HEADS), jnp.tile(wki, (1, 1, IDX_HEADS)),
                            jnp.tile(wki_r, (1, 1, IDX_HEADS)), wwi], axis=-1)
    widx_hi, w1lo = _split_hi_lo(widx)
    w1cat = jnp.concatenate(
        [_pair_major(wq).astype(BF16), _pair_major(_rot_cols(wq, N_HEADS)).astype(BF16), wk.astype(BF16),
         _rot_cols(wk, KV_HEADS).astype(BF16), wv.astype(BF16), widx_hi, wconv.astype(BF16)], axis=-1)
    wpa = jnp.swapaxes(w_pa.reshape(DEPTH, KV_HEADS, 2, HEAD_DIM, D_MODEL), 1, 2).reshape(DEPTH, ATTN_WIDTH, D_MODEL)
    wr, wrl = _split_hi_lo(jnp.pad(w_router, ((0, 0), (0, 0), (0, LANES - N_EXPERTS))))
    br = jnp.pad(b_router, ((0, 0), (0, LANES - N_EXPERTS)), constant_values=-jnp.inf)[:, None, :]
    return w1cat, w1lo, wgate.astype(BF16), wpa.astype(BF16), wr, wrl, br


def _rope_tables():
    pos = jnp.concatenate([jnp.tile(jnp.arange(SEQ, dtype=I32), BATCH),
                           jnp.tile(PAST_LEN + jnp.arange(DEC_SEQ, dtype=I32), DEC_BATCH)])
    half = HEAD_DIM // 2
    inv = ROPE_THETA ** (-jnp.arange(half, dtype=F32) / half)
    ang = pos.astype(F32)[:, None] * inv[None, :]
    return jnp.tile(jnp.cos(ang), (1, LANES // half)), jnp.tile(jnp.sin(ang), (1, LANES // half))


def kernel(x_prompt, x_sample, cache_k, cache_v, cache_kidx, state_conv, page_table, w_in, w_conv, w_pa, w_pb,
           w_o, ln1_g, ln1_b, w_router, b_router, w1, b1, w2, b2, ln2_g, ln2_b):
    w1cat, w1lo, wgate, wpa, wr, wrl, br = _prep_weights(w_in, w_pa, w_router, b_router)
    wpb = w_pb.astype(BF16)
    wo = w_o.astype(BF16)
    wc = jnp.pad(w_conv, ((0, 0), (0, 8 - w_conv.shape[1]), (0, 0)))
    b1g = b1[..., 0::2][:, :, None, :]
    b1l = b1[..., 1::2][:, :, None, :]
    b2r = b2[:, :, None, :]
    cos, sin = _rope_tables()
    ckT = jnp.transpose(cache_k, (0, 1, 3, 4, 2)).reshape(DEPTH, -1, KV_WIDTH, PAGE_SIZE)
    cvT = jnp.transpose(cache_v, (0, 1, 3, 4, 2)).reshape(DEPTH, -1, KV_WIDTH, PAGE_SIZE)
    ckiT = jnp.transpose(cache_kidx, (0, 1, 3, 2))
    pt = page_table.reshape(-1).astype(I32)

    x = jnp.concatenate([x_prompt.reshape(N_P, D_MODEL), x_sample.reshape(N_S, D_MODEL)], axis=0)
    ks, vs, kis, convs = [], [], [], []
    for l in range(DEPTH):
        q, qi, wi, k, v, ki4, kb, vb, kix, cb, u = _inproj(x, w1cat[l], w1lo[l], cos, sin)
        a_p = _attn_prompt(q, qi, wi, kb, vb, kix)
        a_s = _attn_sample(l, pt, q, qi, wi, k, v, ki4, ckT, cvT, ckiT)
        a = jnp.concatenate([a_p, a_s], axis=0)

        up = u[:N_P].reshape(BATCH, SEQ, CONV_DIM)
        us = jnp.concatenate([state_conv[l], u[N_P:].reshape(DEC_BATCH, DEC_SEQ, CONV_DIM)], axis=1)
        zp = jnp.zeros((BATCH, 2, CONV_DIM), F32)
        um1 = jnp.concatenate([jnp.concatenate([zp[:, :1], up[:, :-1]], axis=1).reshape(N_P, CONV_DIM),
                               us[:, 1:1 + DEC_SEQ].reshape(N_S, CONV_DIM)], axis=0)
        um2 = jnp.concatenate([jnp.concatenate([zp, up[:, :-2]], axis=1).reshape(N_P, CONV_DIM),
                               us[:, 0:DEC_SEQ].reshape(N_S, CONV_DIM)], axis=0)

        x1, te, tg = _merge(x, a, cb, u, um1, um2, wgate[l], wpa[l], wpb[l], wo[l], wc[l],
                            ln1_g[l][None], ln1_b[l][None], wr[l], wrl[l], br[l])
        dest, row_tok, blk_e, n_act = _route(te[:, :TOP_K])
        xs = _sc_gather(x1, row_tok)
        y_rows = _moe(l, blk_e, n_act, xs, w1, b1g, b1l, w2, b2r)
        yk = _sc_gather(y_rows, dest.reshape(N_TOK, TOP_K).T.reshape(-1)).reshape(TOP_K, N_TOK, D_MODEL)
        x = _ln2(x1, yk, tg, ln2_g[l][None], ln2_b[l][None])

        ks.append(k)
        vs.append(v)
        kis.append(ki4[:, :IDX_DIM])
        convs.append((up[:, -2:], us[:, -2:]))

    def split(parts, tail):
        st = jnp.stack(parts, axis=0)
        return (st[:, :N_P].reshape(DEPTH, BATCH, SEQ, *tail), st[:, N_P:].reshape(DEPTH, DEC_BATCH, DEC_SEQ, *tail))

    k_p, k_s = split(ks, (KV_HEADS, HEAD_DIM))
    v_p, v_s = split(vs, (KV_HEADS, HEAD_DIM))
    ki_p, ki_s = split(kis, (IDX_DIM,))
    conv_p = jnp.stack([c_[0] for c_ in convs], axis=0)
    conv_s = jnp.stack([c_[1] for c_ in convs], axis=0)
    y_p = x[:N_P].reshape(BATCH, SEQ, D_MODEL)
    y_s = x[N_P:].reshape(DEC_BATCH, DEC_SEQ, D_MODEL)
    return (y_p, y_s, k_p, v_p, ki_p, conv_p, k_s, v_s, ki_s, conv_s)
```

```python
import functools

import numpy as np
import jax
import jax.numpy as jnp
from jax import lax
from jax.experimental import pallas as pl
from jax.experimental.pallas import tpu as pltpu
from jax.experimental.pallas import tpu_sc as plsc

F32 = jnp.float32
BF16 = jnp.bfloat16
I32 = jnp.int32

D_MODEL = 1024
BATCH = 4
SEQ = 4096
DEPTH = 4
DEC_BATCH = 128
DEC_SEQ = 8
PAGE_SIZE = 128
N_PAGES = 16
PAST_LEN = N_PAGES * PAGE_SIZE
N_HEADS = 8
HEAD_DIM = 64
KV_HEADS = 4
IDX_HEADS = 4
IDX_DIM = 64
TOPK = 256
CONV_DIM = 512
N_EXPERTS = 32
TOP_K = 4
D_EXPERT = 1024
SWIGLU_LIMIT = 7.0
SWIGLU_ALPHA = 1.702
ROPE_THETA = 10000.0
LN_EPS = 1e-5
ATTN_WIDTH = N_HEADS * HEAD_DIM
KV_WIDTH = KV_HEADS * HEAD_DIM
IDX_WIDTH = IDX_HEADS * IDX_DIM
DEEPNORM_ALPHA = (2 * DEPTH) ** 0.25

N_P = BATCH * SEQ
N_S = DEC_BATCH * DEC_SEQ
N_TOK = N_P + N_S

LANES = 128
VMEM_LIMIT = 56 * 1024 * 1024
INT_MIN = -2 ** 31

_SEC = {}
_off = 0
for _name, _w in (("q", 512), ("qr", 512), ("k", 256), ("kr", 256), ("v", 256), ("qi", 256), ("qir", 256),
                  ("ki", 256), ("kir", 256), ("wi", 128), ("cb", 512), ("cc", 512), ("cx", 512)):
    _SEC[_name] = (_off, _off + _w)
    _off += _w
W1_COLS = _off
_SEC_LO = {}
_off = 0
for _name in ("qi", "qir", "ki", "kir", "wi"):
    _w = _SEC[_name][1] - _SEC[_name][0]
    _SEC_LO[_name] = (_off, _off + _w)
    _off += _w
W1_LO_COLS = _off

TM = 512
TQ = 256
MOE_BM = 256
MOE_ROWS = N_TOK * TOP_K + N_EXPERTS * MOE_BM
MOE_NBLK = MOE_ROWS // MOE_BM


def _cparams(n_axes):
    return pltpu.CompilerParams(dimension_semantics=("arbitrary",) * n_axes, vmem_limit_bytes=VMEM_LIMIT)


def _const_spec(shape):
    nd = len(shape)
    return pl.BlockSpec(shape, lambda *_: (0,) * nd)


def _split_hi_lo(x):
    hi = lax.bitcast_convert_type(lax.bitcast_convert_type(x, I32) & jnp.int32(-65536), F32)
    return hi.astype(BF16), (x - hi).astype(BF16)


def _inproj_kernel(x_ref, w_ref, wlo_ref, cos_ref, sin_ref, wc_ref, h1_ref, h2_ref,
                   q_ref, qi_ref, wi_ref, k_ref, v_ref, ki_ref, kb_ref, vb_ref, kix_ref, yc_ref, u_ref, tail_ref):
    x = x_ref[...]
    xb = x.astype(BF16)
    xhi, xlo = _split_hi_lo(x)

    def mm(name):
        a, b = _SEC[name]
        if name not in _SEC_LO:
            return jnp.dot(xb, w_ref[:, a:b], preferred_element_type=F32)
        la, lb = _SEC_LO[name]
        return (jnp.dot(xhi, w_ref[:, a:b], preferred_element_type=F32)
                + jnp.dot(xlo, w_ref[:, a:b], preferred_element_type=F32)
                + jnp.dot(xhi, wlo_ref[:, la:lb], preferred_element_type=F32)
                + jnp.dot(xlo, wlo_ref[:, la:lb], preferred_element_type=F32))

    cos = cos_ref[...]
    sin = sin_ref[...]
    cos2 = jnp.concatenate([cos, cos], axis=1)
    sin2 = jnp.concatenate([sin, sin], axis=1)
    cos4 = jnp.concatenate([cos2, cos2], axis=1)
    sin4 = jnp.concatenate([sin2, sin2], axis=1)
    q_ref[...] = mm("q") * cos4 + mm("qr") * sin4
    qi_ref[...] = mm("qi") * cos2 + mm("qir") * sin2
    k = mm("k") * cos2 + mm("kr") * sin2
    k_ref[...] = k
    kb_ref[...] = k.astype(BF16)
    ki = mm("ki") * cos2 + mm("kir") * sin2
    ki_ref[...] = ki
    ki_hi, ki_lo = _split_hi_lo(ki)
    kix_ref[...] = jnp.concatenate([ki_hi[:, :LANES], ki_lo[:, :LANES]], axis=1)
    v = mm("v")
    v_ref[...] = v
    vb_ref[...] = v.astype(BF16)
    wi_ref[...] = mm("wi")

    i = pl.program_id(0)
    u = mm("cc") * mm("cx")
    u_ref[...] = u

    @pl.when(i == 0)
    def _():
        tail_ref[...] = jnp.zeros(tail_ref.shape, F32)

    row = lax.broadcasted_iota(I32, (TM, CONV_DIM), 0)
    back1 = pltpu.roll(u, 1, axis=0)
    back2 = pltpu.roll(u, 2, axis=0)
    mid_seq = lax.rem(i * TM, SEQ) != 0
    tail = jnp.where(mid_seq, tail_ref[...], 0.0)
    p1 = jnp.where(row == 0, tail[7:8], back1)
    p2 = jnp.where(row == 0, tail[6:7], jnp.where(row == 1, tail[7:8], back2))
    pos = lax.rem(row, DEC_SEQ)
    s1 = jnp.where(pos == 0, h1_ref[...], back1)
    s2 = jnp.where(pos < 2, h2_ref[...], back2)
    is_sample = i >= N_P // TM
    um1 = jnp.where(is_sample, s1, p1)
    um2 = jnp.where(is_sample, s2, p2)
    tail_ref[...] = u[TM - 8:, :]
    wc = wc_ref[...]
    yc_ref[...] = mm("cb") * (wc[0:1] * um2 + wc[1:2] * um1 + wc[2:3] * u)


def _inproj(x, w, wlo, cos, sin, wc, h1, h2):
    n = x.shape[0]
    row = lambda w_: pl.BlockSpec((TM, w_), lambda i: (i, 0))
    srow = pl.BlockSpec((TM, CONV_DIM), lambda i: (jnp.maximum(i - N_P // TM, 0), 0))
    widths = (512, 256, 128, 256, 256, 256, 256, 256, 256, 512, 512)
    dtypes = (F32, F32, F32, F32, F32, F32, BF16, BF16, BF16, F32, F32)
    return pl.pallas_call(
        _inproj_kernel,
        grid=(n // TM,),
        in_specs=[row(D_MODEL), _const_spec((D_MODEL, W1_COLS)), _const_spec((D_MODEL, W1_LO_COLS)),
                  row(LANES), row(LANES), _const_spec(wc.shape), srow, srow],
        out_specs=[row(w_) for w_ in widths],
        out_shape=[jax.ShapeDtypeStruct((n, w_), d_) for w_, d_ in zip(widths, dtypes)],
        scratch_shapes=[pltpu.VMEM((8, CONV_DIM), F32)],
        compiler_params=_cparams(1),
        name="inproj",
    )(x, w, wlo, cos, sin, wc, h1, h2)


def _sort_key(score, valid):
    bits = pltpu.bitcast(score, I32)
    key = jnp.where(bits < 0, bits ^ jnp.int32(0x7FFFFFFF), bits)
    key = jnp.where(score == 0.0, jnp.int32(0), key)
    return jnp.where(valid, key, jnp.int32(INT_MIN))


def _count(mask):
    return jnp.sum(jnp.where(mask, 1.0, 0.0), axis=1, keepdims=True)


def _select_topk(key_ref, c0_ref, s_idx, n_keys, two_bits=False):
    rows = key_ref.shape[0]
    kf = float(TOPK)

    def thr_step(it, t):
        cand = t + jnp.left_shift(jnp.int32(1), 31 - it)
        cnt = _count(key_ref[...] >= cand)
        return jnp.where(cnt >= kf, cand, t)

    def thr_step2(it, t):
        d = jnp.left_shift(jnp.int32(1), 30 - 2 * it)
        c1 = t + d
        c2 = c1 + d
        c3 = c2 + d
        key = key_ref[...]
        n1, n2, n3 = _count(key >= c1), _count(key >= c2), _count(key >= c3)
        return jnp.where(n3 >= kf, c3, jnp.where(n2 >= kf, c2, jnp.where(n1 >= kf, c1, t)))

    thr0 = jnp.full((rows, 1), INT_MIN, I32)
    thr = lax.fori_loop(0, 16, thr_step2, thr0) if two_bits else lax.fori_loop(0, 32, thr_step, thr0)
    key = key_ref[...]
    live = key > jnp.int32(INT_MIN)
    n_gt = _count(key > thr)
    n_ge = _count(jnp.logical_and(key >= thr, live))
    c0_ref[...] = jnp.full(c0_ref.shape, n_keys, I32)

    @pl.when(jnp.max(n_ge) > kf)
    def _():
        need = kf - n_gt
        nbits = int(n_keys).bit_length()

        def idx_step(it, c):
            cand = c + jnp.left_shift(jnp.int32(1), nbits - 1 - it)
            eq = jnp.logical_and(key_ref[...] == thr, s_idx < cand)
            return jnp.where(_count(eq) < need, cand, c)

        c_last = lax.fori_loop(0, nbits, idx_step, jnp.zeros((rows, 1), I32))
        c0_ref[...] = jnp.broadcast_to(c_last, c0_ref.shape)

    c0 = c0_ref[:, 0:1]
    tie = jnp.logical_and(key == thr, s_idx <= c0)
    return jnp.logical_and(jnp.logical_or(key > thr, tie), live)


def _head_mask(x, g):
    lane = lax.broadcasted_iota(I32, x.shape, 1)
    return jnp.where(jnp.right_shift(lane, 6) == g, x, 0.0)


_NT = (((1,), (1,)), ((), ()))


def _attn_p_kernel(q_ref, qi_ref, wi_ref, k_ref, v_ref, kix_ref, o_ref, key_ref, neg_ref, c0_ref, *, n_keys, q_base):
    t0 = q_base + pl.program_id(1) * TQ
    qi = qi_ref[...]
    wi = wi_ref[...]
    kix = kix_ref[...]
    lane_blk = jnp.right_shift(lax.broadcasted_iota(I32, (TQ, IDX_WIDTH), 1), 6)
    score = jnp.zeros((TQ, n_keys), F32)
    for h in range(IDX_HEADS):
        x = _head_mask(qi, h)
        x = x + pltpu.roll(x, 2 * IDX_DIM, axis=1)
        x = x + pltpu.roll(x, IDX_DIM, axis=1)
        x_hi, x_lo = _split_hi_lo(x)
        qx = jnp.where((lane_blk & 1) == 0, x_hi, x_lo)
        rel = lax.dot_general(qx, kix, _NT, preferred_element_type=F32)
        score = score + wi[:, h:h + 1] * jnp.maximum(rel, 0.0)
    s_idx = lax.broadcasted_iota(I32, (TQ, n_keys), 1)
    t_idx = t0 + lax.broadcasted_iota(I32, (TQ, n_keys), 0)
    key_ref[...] = _sort_key(score, s_idx <= t_idx)
    sel = _select_topk(key_ref, c0_ref, s_idx, n_keys)
    neg_ref[...] = jnp.where(sel, 0.0, -jnp.inf)

    kb = k_ref[...]
    vb = v_ref[...]
    for hq in range(2):
        qh = q_ref[:, hq * KV_WIDTH:(hq + 1) * KV_WIDTH]
        acc = jnp.zeros((TQ, KV_WIDTH), F32)
        for g in range(KV_HEADS):
            qg = (_head_mask(qh, g) * HEAD_DIM ** -0.5).astype(BF16)
            lg = lax.dot_general(qg, kb, _NT, preferred_element_type=F32) + neg_ref[...]
            p = jnp.exp(lg - jnp.max(lg, axis=1, keepdims=True))
            inv = 1.0 / jnp.sum(p, axis=1, keepdims=True)
            pv = jnp.dot(p.astype(BF16), vb, preferred_element_type=F32)
            acc = acc + _head_mask(pv * inv, g)
        o_ref[:, hq * KV_WIDTH:(hq + 1) * KV_WIDTH] = acc


def _attn_p_group(q, qi, wi, kb, vb, kix, q_lo, q_hi, n_keys):
    nq = (q_hi - q_lo) // TQ
    qrow = lambda w_: pl.BlockSpec((TQ, w_), lambda b, i: ((b * SEQ + q_lo) // TQ + i, 0))
    krow = pl.BlockSpec((None, n_keys, KV_WIDTH), lambda b, i: (b, 0, 0))
    out = pl.pallas_call(
        functools.partial(_attn_p_kernel, n_keys=n_keys, q_base=q_lo),
        grid=(BATCH, nq),
        in_specs=[qrow(ATTN_WIDTH), qrow(IDX_WIDTH), qrow(LANES), krow, krow, krow],
        out_specs=pl.BlockSpec((TQ, ATTN_WIDTH), lambda b, i: (b * nq + i, 0)),
        out_shape=jax.ShapeDtypeStruct((BATCH * (q_hi - q_lo), ATTN_WIDTH), F32),
        scratch_shapes=[pltpu.VMEM((TQ, n_keys), I32), pltpu.VMEM((TQ, n_keys), F32), pltpu.VMEM((TQ, LANES), I32)],
        compiler_params=_cparams(2),
        name=f"attn_p{n_keys}",
    )(q, qi, wi, kb, vb, kix)
    return out.reshape(BATCH, q_hi - q_lo, ATTN_WIDTH)


P_GROUP = 512
_P_GROUPS = tuple((lo, lo + P_GROUP, lo + P_GROUP) for lo in range(0, SEQ, P_GROUP))


def _attn_prompt(q, qi, wi, kb, vb, kix):
    seq3 = lambda a_: a_[:N_P].reshape(BATCH, SEQ, KV_WIDTH)
    kb, vb, kix = seq3(kb), seq3(vb), seq3(kix)
    parts = [_attn_p_group(q, qi, wi, kb, vb, kix, lo, hi, nk) for lo, hi, nk in _P_GROUPS]
    return jnp.concatenate(parts, axis=1).reshape(N_P, ATTN_WIDTH)


S_KEYS = PAST_LEN + LANES


def _attn_s_kernel(pt_ref, q_ref, qi_ref, wi_ref, kn_ref, vn_ref, kin_ref, *rest):
    kT = rest[0:N_PAGES]
    vT = rest[N_PAGES:2 * N_PAGES]
    kiT = rest[2 * N_PAGES:3 * N_PAGES]
    o_ref, kTs, vTs, kiTs, key_ref, c0_ref = rest[3 * N_PAGES:]
    del pt_ref
    for j in range(N_PAGES):
        sl = slice(j * PAGE_SIZE, (j + 1) * PAGE_SIZE)
        kTs[:, sl] = kT[j][...].astype(BF16)
        vTs[:, sl] = vT[j][...].astype(BF16)
        kiTs[:, sl] = kiT[j][...].astype(BF16)

    pad = jnp.zeros((LANES - DEC_SEQ, KV_WIDTH), F32)
    kn = jnp.concatenate([kn_ref[...], pad], axis=0).astype(BF16)
    vn = jnp.concatenate([vn_ref[...], pad], axis=0).astype(BF16)
    kin = jnp.concatenate([kin_ref[...], pad], axis=0)[:, :IDX_DIM].astype(BF16)

    qi = qi_ref[...]
    wi = wi_ref[...]
    qis = jnp.concatenate([qi[:, h * IDX_DIM:(h + 1) * IDX_DIM] for h in range(IDX_HEADS)], axis=0).astype(BF16)
    rel = jnp.concatenate([jnp.dot(qis, kiTs[...], preferred_element_type=F32),
                           lax.dot_general(qis, kin, _NT, preferred_element_type=F32)], axis=1)
    rel = jnp.maximum(rel, 0.0)
    score = jnp.zeros((DEC_SEQ, S_KEYS), F32)
    for h in range(IDX_HEADS):
        score = score + wi[:, h:h + 1] * rel[h * DEC_SEQ:(h + 1) * DEC_SEQ]
    s_idx = lax.broadcasted_iota(I32, (DEC_SEQ, S_KEYS), 1)
    t_idx = PAST_LEN + lax.broadcasted_iota(I32, (DEC_SEQ, S_KEYS), 0)
    key_ref[...] = _sort_key(score, s_idx <= t_idx)
    sel = _select_topk(key_ref, c0_ref, s_idx, S_KEYS, two_bits=True)
    neg = jnp.where(sel, 0.0, -jnp.inf)

    q = q_ref[...]
    qs = jnp.concatenate([_head_mask(q[:, hq * KV_WIDTH:(hq + 1) * KV_WIDTH], g)
                          for hq in range(2) for g in range(KV_HEADS)], axis=0).astype(BF16)
    lg = jnp.concatenate([jnp.dot(qs, kTs[...], preferred_element_type=F32),
                          lax.dot_general(qs, kn, _NT, preferred_element_type=F32)], axis=1)
    lg = lg * HEAD_DIM ** -0.5 + jnp.concatenate([neg] * N_HEADS, axis=0)
    p = jnp.exp(lg - jnp.max(lg, axis=1, keepdims=True))
    inv = 1.0 / jnp.sum(p, axis=1, keepdims=True)
    pb = p.astype(BF16)
    pv = (lax.dot_general(pb[:, :PAST_LEN], vTs[...], _NT, preferred_element_type=F32)
          + jnp.dot(pb[:, PAST_LEN:], vn, preferred_element_type=F32)) * inv
    for hq in range(2):
        acc = jnp.zeros((DEC_SEQ, KV_WIDTH), F32)
        for g in range(KV_HEADS):
            r = (hq * KV_HEADS + g) * DEC_SEQ
            acc = acc + _head_mask(pv[r:r + DEC_SEQ], g)
        o_ref[:, hq * KV_WIDTH:(hq + 1) * KV_WIDTH] = acc


def _attn_sample(layer, page_table, q, qi, wi, k, v, ki, ckT, cvT, ckiT):
    srow = lambda w_: pl.BlockSpec((DEC_SEQ, w_), lambda b, pt: (N_P // DEC_SEQ + b, 0))

    def page(rows, j):
        return pl.BlockSpec((None, None, rows, PAGE_SIZE), lambda b, pt: (layer, pt[b * N_PAGES + j], 0, 0))

    in_specs = [srow(ATTN_WIDTH), srow(IDX_WIDTH), srow(LANES), srow(KV_WIDTH), srow(KV_WIDTH), srow(KV_WIDTH)]
    in_specs += [page(KV_WIDTH, j) for j in range(N_PAGES)]
    in_specs += [page(KV_WIDTH, j) for j in range(N_PAGES)]
    in_specs += [page(IDX_DIM, j) for j in range(N_PAGES)]
    grid_spec = pltpu.PrefetchScalarGridSpec(
        num_scalar_prefetch=1,
        grid=(DEC_BATCH,),
        in_specs=in_specs,
        out_specs=pl.BlockSpec((DEC_SEQ, ATTN_WIDTH), lambda b, pt: (b, 0)),
        scratch_shapes=[pltpu.VMEM((KV_WIDTH, PAST_LEN), BF16), pltpu.VMEM((KV_WIDTH, PAST_LEN), BF16),
                        pltpu.VMEM((IDX_DIM, PAST_LEN), BF16), pltpu.VMEM((DEC_SEQ, S_KEYS), I32),
                        pltpu.VMEM((DEC_SEQ, LANES), I32)],
    )
    return pl.pallas_call(
        _attn_s_kernel,
        grid_spec=grid_spec,
        out_shape=jax.ShapeDtypeStruct((N_S, ATTN_WIDTH), F32),
        compiler_params=_cparams(1),
        name="attn_s",
    )(page_table, q, qi, wi, k, v, ki, *([ckT] * N_PAGES), *([cvT] * N_PAGES), *([ckiT] * N_PAGES))


def _layer_norm(y, g, b):
    mu = jnp.mean(y, axis=1, keepdims=True)
    d = y - mu
    var = jnp.mean(d * d, axis=1, keepdims=True)
    return d * lax.rsqrt(var + LN_EPS) * g + b


def _merge_kernel(x_ref, a_ref, yc_ref, wg_ref, wpa_ref, wpb_ref, wo_ref,
                  g_ref, b_ref, wr_ref, wrl_ref, br_ref, x1_ref, te_ref, tg_ref):
    x = x_ref[...]
    gates = jax.nn.sigmoid(jnp.dot(x.astype(BF16), wg_ref[...], preferred_element_type=F32))
    pa = jnp.dot(a_ref[...].astype(BF16), wpa_ref[...], preferred_element_type=F32)
    pb = jnp.dot(yc_ref[...].astype(BF16), wpb_ref[...], preferred_element_type=F32)
    merged = gates[:, :D_MODEL] * pa + gates[:, D_MODEL:] * pb
    mix = jnp.dot(merged.astype(BF16), wo_ref[...], preferred_element_type=F32)
    x1 = _layer_norm(DEEPNORM_ALPHA * x + mix, g_ref[...], b_ref[...])
    x1_ref[...] = x1

    x1h, x1l = _split_hi_lo(x1)
    wrh = wr_ref[...]
    wrl = wrl_ref[...]
    lg = (jnp.dot(x1h, wrh, preferred_element_type=F32) + jnp.dot(x1l, wrh, preferred_element_type=F32)
          + jnp.dot(x1h, wrl, preferred_element_type=F32) + jnp.dot(x1l, wrl, preferred_element_type=F32)
          + br_ref[...])
    lane = lax.broadcasted_iota(I32, lg.shape, 1)
    te = jnp.zeros(lg.shape, I32)
    vals = []
    for r in range(TOP_K):
        m = jnp.max(lg, axis=1, keepdims=True)
        idx = jnp.min(jnp.where(lg == m, lane, LANES), axis=1, keepdims=True)
        te = jnp.where(lane == r, idx, te)
        vals.append(m)
        lg = jnp.where(lane == idx, -jnp.inf, lg)
    ex = [jnp.exp(v_ - vals[0]) for v_ in vals]
    inv = 1.0 / (ex[0] + ex[1] + ex[2] + ex[3])
    tg = jnp.zeros(lg.shape, F32)
    for r in range(TOP_K):
        tg = jnp.where(lane == r, ex[r] * inv, tg)
    te_ref[...] = te
    tg_ref[...] = tg


def _merge(x, a, yc, wg, wpa, wpb, wo, g, b, wr, wrl, br):
    n = x.shape[0]
    row = lambda w_: pl.BlockSpec((TM, w_), lambda i: (i, 0))
    return pl.pallas_call(
        _merge_kernel,
        grid=(n // TM,),
        in_specs=[row(D_MODEL), row(ATTN_WIDTH), row(CONV_DIM),
                  _const_spec(wg.shape), _const_spec(wpa.shape), _const_spec(wpb.shape), _const_spec(wo.shape),
                  _const_spec(g.shape), _const_spec(b.shape),
                  _const_spec(wr.shape), _const_spec(wrl.shape), _const_spec(br.shape)],
        out_specs=[row(D_MODEL), row(LANES), row(LANES)],
        out_shape=[jax.ShapeDtypeStruct((n, D_MODEL), F32),
                   jax.ShapeDtypeStruct((n, LANES), I32), jax.ShapeDtypeStruct((n, LANES), F32)],
        compiler_params=_cparams(1),
        name="merge",
    )(x, a, yc, wg, wpa, wpb, wo, g, b, wr, wrl, br)


PAIR_W = 2 * LANES


def _moe_kernel(be_ref, na_ref, nv_ref, xs_ref, w1_ref, b1g_ref, b1l_ref, w2_ref, b2_ref, sel_ref, o_ref,
                w1g_s, w1l_s, w2_s):
    i = pl.program_id(0)
    active = i < na_ref[0]
    new_expert = jnp.logical_or(i == 0, be_ref[i] != be_ref[jnp.maximum(i - 1, 0)])

    @pl.when(jnp.logical_and(active, new_expert))
    def _():
        sel = sel_ref[...]
        for c in range(2 * D_EXPERT // PAIR_W):
            wc = w1_ref[:, c * PAIR_W:(c + 1) * PAIR_W].astype(BF16)
            d = jnp.dot(wc, sel, preferred_element_type=F32).astype(BF16)
            w1g_s[:, c * LANES:(c + 1) * LANES] = d[:, :LANES]
            w1l_s[:, c * LANES:(c + 1) * LANES] = d[:, LANES:]
        w2_s[...] = w2_ref[...].astype(BF16)

    @pl.when(active)
    def _():
        row = lax.broadcasted_iota(I32, (MOE_BM, D_MODEL), 0)
        xs = jnp.where(row < nv_ref[i], xs_ref[...], 0.0).astype(BF16)
        hg = jnp.dot(xs, w1g_s[...], preferred_element_type=F32) + b1g_ref[...]
        hl = jnp.dot(xs, w1l_s[...], preferred_element_type=F32) + b1l_ref[...]
        glu = jnp.minimum(hg, SWIGLU_LIMIT)
        lin = jnp.clip(hl, -SWIGLU_LIMIT, SWIGLU_LIMIT)
        act = glu * jax.nn.sigmoid(SWIGLU_ALPHA * glu) * (lin + 1.0)
        o_ref[...] = jnp.dot(act.astype(BF16), w2_s[...], preferred_element_type=F32) + b2_ref[...]

    @pl.when(jnp.logical_not(active))
    def _():
        o_ref[...] = jnp.zeros(o_ref.shape, F32)


def _pair_select():
    r = np.arange(PAIR_W)[:, None]
    c = np.arange(PAIR_W)[None, :]
    src = np.where(c < LANES, 2 * c, 2 * (c - LANES) + 1)
    return jnp.asarray(r == src, BF16)


def _moe(layer, blk_e, n_act, n_valid, xs, w1, b1g, b1l, w2, b2):
    wspec = lambda r, c: pl.BlockSpec((None, None, r, c), lambda i, be, na, nv: (layer, be[i], 0, 0))
    grid_spec = pltpu.PrefetchScalarGridSpec(
        num_scalar_prefetch=3,
        grid=(MOE_NBLK,),
        in_specs=[pl.BlockSpec((MOE_BM, D_MODEL), lambda i, be, na, nv: (i, 0)),
                  wspec(D_MODEL, 2 * D_EXPERT), wspec(1, D_EXPERT), wspec(1, D_EXPERT),
                  wspec(D_EXPERT, D_MODEL), wspec(1, D_MODEL),
                  pl.BlockSpec((PAIR_W, PAIR_W), lambda i, be, na, nv: (0, 0))],
        out_specs=pl.BlockSpec((MOE_BM, D_MODEL), lambda i, be, na, nv: (i, 0)),
        scratch_shapes=[pltpu.VMEM((D_MODEL, D_EXPERT), BF16), pltpu.VMEM((D_MODEL, D_EXPERT), BF16),
                        pltpu.VMEM((D_EXPERT, D_MODEL), BF16)],
    )
    return pl.pallas_call(
        _moe_kernel,
        grid_spec=grid_spec,
        out_shape=jax.ShapeDtypeStruct((MOE_ROWS, D_MODEL), F32),
        compiler_params=_cparams(1),
        name="moe",
    )(blk_e, n_act, n_valid, xs, w1, b1g, b1l, w2, b2, _pair_select())


SC_ROWS = 64


def _sc_gather(table, idx):
    n_idx = idx.shape[0]
    width = table.shape[1]
    info = plsc.get_sparse_core_info()
    n_cores = info.num_cores
    n_workers = n_cores * info.num_subcores
    per_worker = n_idx // n_workers
    assert per_worker * n_workers == n_idx and per_worker % SC_ROWS == 0
    mesh = plsc.VectorSubcoreMesh(core_axis_name="c", subcore_axis_name="s")

    @functools.partial(
        pl.kernel, mesh=mesh,
        out_type=jax.ShapeDtypeStruct((n_idx, width), table.dtype),
        scratch_types=[pltpu.VMEM((SC_ROWS,), I32), pltpu.VMEM((SC_ROWS, width), table.dtype),
                       pltpu.SemaphoreType.DMA],
    )
    def gather_kernel(table_hbm, idx_hbm, out_hbm, idx_v, rows_v, sem):
        base = (lax.axis_index("s") * n_cores + lax.axis_index("c")) * per_worker

        @pl.loop(0, per_worker // SC_ROWS)
        def _(j):
            off = base + j * SC_ROWS
            pltpu.sync_copy(idx_hbm.at[pl.ds(off, SC_ROWS)], idx_v)
            pltpu.async_copy(table_hbm.at[idx_v], rows_v, sem).wait()
            pltpu.sync_copy(rows_v, out_hbm.at[pl.ds(off, SC_ROWS)])

    return gather_kernel(table, idx)


SC_TOKENS = 32


def _sc_scatter_rows(x, dest_t, n_rows):
    n_tok, width = x.shape
    top_k = dest_t.shape[0]
    info = plsc.get_sparse_core_info()
    n_cores = info.num_cores
    n_workers = n_cores * info.num_subcores
    per_worker = n_tok // n_workers
    assert per_worker * n_workers == n_tok and per_worker % SC_TOKENS == 0
    mesh = plsc.VectorSubcoreMesh(core_axis_name="c", subcore_axis_name="s")

    @functools.partial(
        pl.kernel, mesh=mesh,
        out_type=jax.ShapeDtypeStruct((n_rows, width), x.dtype),
        scratch_types=[pltpu.VMEM((SC_TOKENS,), I32), pltpu.VMEM((SC_TOKENS, width), x.dtype)],
    )
    def scatter_kernel(x_hbm, idx_hbm, out_hbm, idx_v, rows_v):
        base = (lax.axis_index("s") * n_cores + lax.axis_index("c")) * per_worker

        @pl.loop(0, per_worker // SC_TOKENS)
        def _(j):
            t0 = base + j * SC_TOKENS
            pltpu.sync_copy(x_hbm.at[pl.ds(t0, SC_TOKENS)], rows_v)
            for k in range(top_k):
                pltpu.sync_copy(idx_hbm.at[pl.ds(k * n_tok + t0, SC_TOKENS)], idx_v)
                pltpu.sync_copy(rows_v, out_hbm.at[idx_v])

    return scatter_kernel(x, dest_t.reshape(-1))


def _route(top_e):
    flat_e = top_e.reshape(-1)
    onehot = (flat_e[:, None] == jnp.arange(N_EXPERTS, dtype=I32)[None, :]).astype(I32)
    csum = jnp.cumsum(onehot, axis=0)
    counts = csum[-1]
    rank = jnp.take_along_axis(csum, flat_e[:, None], axis=1)[:, 0] - 1
    padded = (counts + MOE_BM - 1) // MOE_BM * MOE_BM
    pad_end = jnp.cumsum(padded)
    pad_start = pad_end - padded
    dest_t = (pad_start[flat_e] + rank).reshape(N_TOK, TOP_K).T
    blk_start = jnp.arange(MOE_NBLK, dtype=I32) * MOE_BM
    blk_e = jnp.minimum(jnp.sum(blk_start[:, None] >= pad_end[None, :], axis=1), N_EXPERTS - 1).astype(I32)
    n_act = (pad_end[-1] // MOE_BM).astype(I32).reshape(1)
    n_valid = jnp.clip((pad_start + counts)[blk_e] - blk_start, 0, MOE_BM).astype(I32)
    return dest_t, blk_e, n_act, n_valid


def _ln2_kernel(x_ref, y_ref, tg_ref, g_ref, b_ref, o_ref):
    tg = tg_ref[...]
    y = tg[:, 0:1] * y_ref[0]
    for r in range(1, TOP_K):
        y = y + tg[:, r:r + 1] * y_ref[r]
    o_ref[...] = _layer_norm(DEEPNORM_ALPHA * x_ref[...] + y, g_ref[...], b_ref[...])


def _ln2(x1, yk, tg, g, b):
    n = x1.shape[0]
    return pl.pallas_call(
        _ln2_kernel,
        grid=(n // TM,),
        in_specs=[pl.BlockSpec((TM, D_MODEL), lambda i: (i, 0)),
                  pl.BlockSpec((TOP_K, TM, D_MODEL), lambda i: (0, i, 0)),
                  pl.BlockSpec((TM, LANES), lambda i: (i, 0)),
                  _const_spec(g.shape), _const_spec(b.shape)],
        out_specs=pl.BlockSpec((TM, D_MODEL), lambda i: (i, 0)),
        out_shape=jax.ShapeDtypeStruct((n, D_MODEL), F32),
        compiler_params=_cparams(1),
        name="ln2",
    )(x1, yk, tg, g, b)


def _rot_cols(w, n_heads):
    lead = w.shape[:-1]
    w4 = w.reshape(*lead, n_heads, 2, HEAD_DIM // 2)
    return jnp.concatenate([-w4[..., 1, :], w4[..., 0, :]], axis=-1).reshape(*lead, n_heads * HEAD_DIM)


def _pair_major(w):
    lead = w.shape[:-1]
    w4 = w.reshape(*lead, KV_HEADS, 2, HEAD_DIM)
    return jnp.swapaxes(w4, -3, -2).reshape(*lead, ATTN_WIDTH)


def _prep_weights(w_in, w_pa, w_router, b_router):
    c = np.cumsum((ATTN_WIDTH, KV_WIDTH, KV_WIDTH, IDX_WIDTH, IDX_DIM, IDX_HEADS,
                   CONV_DIM, CONV_DIM, CONV_DIM, D_MODEL, D_MODEL)).tolist()
    wq, wk, wv = w_in[..., :c[0]], w_in[..., c[0]:c[1]], w_in[..., c[1]:c[2]]
    wqi, wki, wwi = w_in[..., c[2]:c[3]], w_in[..., c[3]:c[4]], w_in[..., c[4]:c[5]]
    wconv = w_in[..., c[5]:c[8]]
    wgate = w_in[..., c[8]:]
    wki_r = _rot_cols(wki, 1)
    wwi = jnp.pad(wwi * (IDX_HEADS ** -0.5 * IDX_DIM ** -0.5), ((0, 0), (0, 0), (0, LANES - IDX_HEADS)))
    widx = jnp.concatenate([wqi, _rot_cols(wqi, IDX_HEADS), jnp.tile(wki, (1, 1, IDX_HEADS)),
                            jnp.tile(wki_r, (1, 1, IDX_HEADS)), wwi], axis=-1)
    widx_hi, w1lo = _split_hi_lo(widx)
    w1cat = jnp.concatenate(
        [_pair_major(wq).astype(BF16), _pair_major(_rot_cols(wq, N_HEADS)).astype(BF16), wk.astype(BF16),
         _rot_cols(wk, KV_HEADS).astype(BF16), wv.astype(BF16), widx_hi, wconv.astype(BF16)], axis=-1)
    wpa = jnp.swapaxes(w_pa.reshape(DEPTH, KV_HEADS, 2, HEAD_DIM, D_MODEL), 1, 2).reshape(DEPTH, ATTN_WIDTH, D_MODEL)
    wr, wrl = _split_hi_lo(jnp.pad(w_router, ((0, 0), (0, 0), (0, LANES - N_EXPERTS))))
    br = jnp.pad(b_router, ((0, 0), (0, LANES - N_EXPERTS)), constant_values=-jnp.inf)[:, None, :]
    return w1cat, w1lo, wgate.astype(BF16), wpa.astype(BF16), wr, wrl, br


def _rope_tables():
    pos = jnp.concatenate([jnp.tile(jnp.arange(SEQ, dtype=I32), BATCH),
                           jnp.tile(PAST_LEN + jnp.arange(DEC_SEQ, dtype=I32), DEC_BATCH)])
    half = HEAD_DIM // 2
    inv = ROPE_THETA ** (-jnp.arange(half, dtype=F32) / half)
    ang = pos.astype(F32)[:, None] * inv[None, :]
    return jnp.tile(jnp.cos(ang), (1, LANES // half)), jnp.tile(jnp.sin(ang), (1, LANES // half))


def kernel(x_prompt, x_sample, cache_k, cache_v, cache_kidx, state_conv, page_table, w_in, w_conv, w_pa, w_pb,
           w_o, ln1_g, ln1_b, w_router, b_router, w1, b1, w2, b2, ln2_g, ln2_b):
    w1cat, w1lo, wgate, wpa, wr, wrl, br = _prep_weights(w_in, w_pa, w_router, b_router)
    wpb = w_pb.astype(BF16)
    wo = w_o.astype(BF16)
    wc = jnp.pad(w_conv, ((0, 0), (0, 8 - w_conv.shape[1]), (0, 0)))
    b1g = b1[..., 0::2][:, :, None, :]
    b1l = b1[..., 1::2][:, :, None, :]
    b2r = b2[:, :, None, :]
    cos, sin = _rope_tables()
    zrows = jnp.zeros((DEPTH, DEC_BATCH, DEC_SEQ - 2, CONV_DIM), F32)
    h1 = jnp.concatenate([state_conv[:, :, 1:2], zrows[:, :, :1], zrows], axis=2).reshape(DEPTH, N_S, CONV_DIM)
    h2 = jnp.concatenate([state_conv, zrows], axis=2).reshape(DEPTH, N_S, CONV_DIM)
    ckT = jnp.transpose(cache_k, (0, 1, 3, 4, 2)).reshape(DEPTH, -1, KV_WIDTH, PAGE_SIZE)
    cvT = jnp.transpose(cache_v, (0, 1, 3, 4, 2)).reshape(DEPTH, -1, KV_WIDTH, PAGE_SIZE)
    ckiT = jnp.transpose(cache_kidx, (0, 1, 3, 2))
    pt = page_table.reshape(-1).astype(I32)

    x = jnp.concatenate([x_prompt.reshape(N_P, D_MODEL), x_sample.reshape(N_S, D_MODEL)], axis=0)
    ks, vs, kis, convs = [], [], [], []
    for l in range(DEPTH):
        q, qi, wi, k, v, ki4, kb, vb, kix, yc, u = _inproj(x, w1cat[l], w1lo[l], cos, sin, wc[l], h1[l], h2[l])
        a_p = _attn_prompt(q, qi, wi, kb, vb, kix)
        a_s = _attn_sample(l, pt, q, qi, wi, k, v, ki4, ckT, cvT, ckiT)
        a = jnp.concatenate([a_p, a_s], axis=0)

        x1, te, tg = _merge(x, a, yc, wgate[l], wpa[l], wpb[l], wo[l],
                            ln1_g[l][None], ln1_b[l][None], wr[l], wrl[l], br[l])
        dest_t, blk_e, n_act, n_valid = _route(te[:, :TOP_K])
        xs = _sc_scatter_rows(x1, dest_t, MOE_ROWS)
        y_rows = _moe(l, blk_e, n_act, n_valid, xs, w1, b1g, b1l, w2, b2r)
        yk = _sc_gather(y_rows, dest_t.reshape(-1)).reshape(TOP_K, N_TOK, D_MODEL)
        x = _ln2(x1, yk, tg, ln2_g[l][None], ln2_b[l][None])

        ks.append(k)
        vs.append(v)
        kis.append(ki4[:, :IDX_DIM])
        convs.append((u[:N_P].reshape(BATCH, SEQ, CONV_DIM)[:, -2:],
                      u[N_P:].reshape(DEC_BATCH, DEC_SEQ, CONV_DIM)[:, -2:]))

    def split(parts, tail):
        st = jnp.stack(parts, axis=0)
        return (st[:, :N_P].reshape(DEPTH, BATCH, SEQ, *tail), st[:, N_P:].reshape(DEPTH, DEC_BATCH, DEC_SEQ, *tail))

    k_p, k_s = split(ks, (KV_HEADS, HEAD_DIM))
    v_p, v_s = split(vs, (KV_HEADS, HEAD_DIM))
    ki_p, ki_s = split(kis, (IDX_DIM,))
    conv_p = jnp.stack([c_[0] for c_ in convs], axis=0)
    conv_s = jnp.stack([c_[1] for c_ in convs], axis=0)
    y_p = x[:N_P].reshape(BATCH, SEQ, D_MODEL)
    y_s = x[N_P:].reshape(DEC_BATCH, DEC_SEQ, D_MODEL)
    return (y_p, y_s, k_p, v_p, ki_p, conv_p, k_s, v_s, ki_s, conv_s)
```

```python
import functools

import numpy as np
import jax
import jax.numpy as jnp
from jax import lax
from jax.experimental import pallas as pl
from jax.experimental.pallas import tpu as pltpu
from jax.experimental.pallas import tpu_sc as plsc

F32 = jnp.float32
BF16 = jnp.bfloat16
I32 = jnp.int32

D_MODEL = 1024
BATCH = 4
SEQ = 4096
DEPTH = 4
DEC_BATCH = 128
DEC_SEQ = 8
PAGE_SIZE = 128
N_PAGES = 16
PAST_LEN = N_PAGES * PAGE_SIZE
N_HEADS = 8
HEAD_DIM = 64
KV_HEADS = 4
IDX_HEADS = 4
IDX_DIM = 64
TOPK = 256
CONV_DIM = 512
N_EXPERTS = 32
TOP_K = 4
D_EXPERT = 1024
SWIGLU_LIMIT = 7.0
SWIGLU_ALPHA = 1.702
ROPE_THETA = 10000.0
LN_EPS = 1e-5
ATTN_WIDTH = N_HEADS * HEAD_DIM
KV_WIDTH = KV_HEADS * HEAD_DIM
IDX_WIDTH = IDX_HEADS * IDX_DIM
DEEPNORM_ALPHA = (2 * DEPTH) ** 0.25

N_P = BATCH * SEQ
N_S = DEC_BATCH * DEC_SEQ
N_TOK = N_P + N_S

LANES = 128
VMEM_LIMIT = 56 * 1024 * 1024
INT_MIN = -2 ** 31

_SEC = {}
_off = 0
for _name, _w in (("q", 512), ("qr", 512), ("k", 256), ("kr", 256), ("v", 256), ("qi", 256), ("qir", 256),
                  ("ki", 256), ("kir", 256), ("wi", 128), ("cb", 512), ("cc", 512), ("cx", 512)):
    _SEC[_name] = (_off, _off + _w)
    _off += _w
W1_COLS = _off
_SEC_LO = {}
_off = 0
for _name in ("qi", "qir", "ki", "kir", "wi"):
    _w = _SEC[_name][1] - _SEC[_name][0]
    _SEC_LO[_name] = (_off, _off + _w)
    _off += _w
W1_LO_COLS = _off

TM = 512
TQ = 256
MOE_BM = 256
MOE_ROWS = N_TOK * TOP_K + N_EXPERTS * MOE_BM
MOE_NBLK = MOE_ROWS // MOE_BM


def _cparams(n_axes):
    return pltpu.CompilerParams(dimension_semantics=("arbitrary",) * n_axes, vmem_limit_bytes=VMEM_LIMIT)


def _const_spec(shape):
    nd = len(shape)
    return pl.BlockSpec(shape, lambda *_: (0,) * nd)


def _split_hi_lo(x):
    hi = lax.bitcast_convert_type(lax.bitcast_convert_type(x, I32) & jnp.int32(-65536), F32)
    return hi.astype(BF16), (x - hi).astype(BF16)


def _inproj_kernel(x_ref, w_ref, wlo_ref, cos_ref, sin_ref, wc_ref, h1_ref, h2_ref,
                   q_ref, qi_ref, wi_ref, k_ref, v_ref, ki_ref, kb_ref, vb_ref, kix_ref, yc_ref, u_ref, tail_ref):
    x = x_ref[...]
    xb = x.astype(BF16)
    xhi, xlo = _split_hi_lo(x)

    def mm(name):
        a, b = _SEC[name]
        if name not in _SEC_LO:
            return jnp.dot(xb, w_ref[:, a:b], preferred_element_type=F32)
        la, lb = _SEC_LO[name]
        return (jnp.dot(xhi, w_ref[:, a:b], preferred_element_type=F32)
                + jnp.dot(xlo, w_ref[:, a:b], preferred_element_type=F32)
                + jnp.dot(xhi, wlo_ref[:, la:lb], preferred_element_type=F32)
                + jnp.dot(xlo, wlo_ref[:, la:lb], preferred_element_type=F32))

    cos = cos_ref[...]
    sin = sin_ref[...]
    cos2 = jnp.concatenate([cos, cos], axis=1)
    sin2 = jnp.concatenate([sin, sin], axis=1)
    cos4 = jnp.concatenate([cos2, cos2], axis=1)
    sin4 = jnp.concatenate([sin2, sin2], axis=1)
    q_ref[...] = mm("q") * cos4 + mm("qr") * sin4
    qi_ref[...] = mm("qi") * cos2 + mm("qir") * sin2
    k = mm("k") * cos2 + mm("kr") * sin2
    k_ref[...] = k
    kb_ref[...] = k.astype(BF16)
    ki = mm("ki") * cos2 + mm("kir") * sin2
    ki_ref[...] = ki
    ki_hi, ki_lo = _split_hi_lo(ki)
    kix_ref[...] = jnp.concatenate([ki_hi[:, :LANES], ki_lo[:, :LANES]], axis=1)
    v = mm("v")
    v_ref[...] = v
    vb_ref[...] = v.astype(BF16)
    wi_ref[...] = mm("wi")

    i = pl.program_id(0)
    u = mm("cc") * mm("cx")
    u_ref[...] = u

    @pl.when(i == 0)
    def _():
        tail_ref[...] = jnp.zeros(tail_ref.shape, F32)

    row = lax.broadcasted_iota(I32, (TM, CONV_DIM), 0)
    back1 = pltpu.roll(u, 1, axis=0)
    back2 = pltpu.roll(u, 2, axis=0)
    mid_seq = lax.rem(i * TM, SEQ) != 0
    tail = jnp.where(mid_seq, tail_ref[...], 0.0)
    p1 = jnp.where(row == 0, tail[7:8], back1)
    p2 = jnp.where(row == 0, tail[6:7], jnp.where(row == 1, tail[7:8], back2))
    pos = lax.rem(row, DEC_SEQ)
    s1 = jnp.where(pos == 0, h1_ref[...], back1)
    s2 = jnp.where(pos < 2, h2_ref[...], back2)
    is_sample = i >= N_P // TM
    um1 = jnp.where(is_sample, s1, p1)
    um2 = jnp.where(is_sample, s2, p2)
    tail_ref[...] = u[TM - 8:, :]
    wc = wc_ref[...]
    yc_ref[...] = mm("cb") * (wc[0:1] * um2 + wc[1:2] * um1 + wc[2:3] * u)


def _inproj(x, w, wlo, cos, sin, wc, h1, h2):
    n = x.shape[0]
    row = lambda w_: pl.BlockSpec((TM, w_), lambda i: (i, 0))
    srow = pl.BlockSpec((TM, CONV_DIM), lambda i: (jnp.maximum(i - N_P // TM, 0), 0))
    widths = (512, 256, 128, 256, 256, 256, 256, 256, 256, 512, 512)
    dtypes = (F32, F32, F32, F32, F32, F32, BF16, BF16, BF16, F32, F32)
    return pl.pallas_call(
        _inproj_kernel,
        grid=(n // TM,),
        in_specs=[row(D_MODEL), _const_spec((D_MODEL, W1_COLS)), _const_spec((D_MODEL, W1_LO_COLS)),
                  row(LANES), row(LANES), _const_spec(wc.shape), srow, srow],
        out_specs=[row(w_) for w_ in widths],
        out_shape=[jax.ShapeDtypeStruct((n, w_), d_) for w_, d_ in zip(widths, dtypes)],
        scratch_shapes=[pltpu.VMEM((8, CONV_DIM), F32)],
        compiler_params=_cparams(1),
        name="inproj",
    )(x, w, wlo, cos, sin, wc, h1, h2)


def _sort_key(score, valid):
    bits = pltpu.bitcast(score, I32)
    key = jnp.where(bits < 0, bits ^ jnp.int32(0x7FFFFFFF), bits)
    key = jnp.where(score == 0.0, jnp.int32(0), key)
    return jnp.where(valid, key, jnp.int32(INT_MIN))


def _count(mask):
    return jnp.sum(jnp.where(mask, 1.0, 0.0), axis=1, keepdims=True)


def _select_topk(key_ref, c0_ref, s_idx, n_keys, two_bits=False):
    rows = key_ref.shape[0]
    kf = float(TOPK)

    def thr_step(it, t):
        cand = t + jnp.left_shift(jnp.int32(1), 31 - it)
        cnt = _count(key_ref[...] >= cand)
        return jnp.where(cnt >= kf, cand, t)

    def thr_step2(it, t):
        d = jnp.left_shift(jnp.int32(1), 30 - 2 * it)
        c1 = t + d
        c2 = c1 + d
        c3 = c2 + d
        key = key_ref[...]
        n1, n2, n3 = _count(key >= c1), _count(key >= c2), _count(key >= c3)
        return jnp.where(n3 >= kf, c3, jnp.where(n2 >= kf, c2, jnp.where(n1 >= kf, c1, t)))

    thr0 = jnp.full((rows, 1), INT_MIN, I32)
    thr = lax.fori_loop(0, 16, thr_step2, thr0) if two_bits else lax.fori_loop(0, 32, thr_step, thr0)
    key = key_ref[...]
    live = key > jnp.int32(INT_MIN)
    n_gt = _count(key > thr)
    n_ge = _count(jnp.logical_and(key >= thr, live))
    c0_ref[...] = jnp.full(c0_ref.shape, n_keys, I32)

    @pl.when(jnp.max(n_ge) > kf)
    def _():
        need = kf - n_gt
        nbits = int(n_keys).bit_length()

        def idx_step(it, c):
            cand = c + jnp.left_shift(jnp.int32(1), nbits - 1 - it)
            eq = jnp.logical_and(key_ref[...] == thr, s_idx < cand)
            return jnp.where(_count(eq) < need, cand, c)

        c_last = lax.fori_loop(0, nbits, idx_step, jnp.zeros((rows, 1), I32))
        c0_ref[...] = jnp.broadcast_to(c_last, c0_ref.shape)

    c0 = c0_ref[:, 0:1]
    tie = jnp.logical_and(key == thr, s_idx <= c0)
    return jnp.logical_and(jnp.logical_or(key > thr, tie), live)


def _head_mask(x, g):
    lane = lax.broadcasted_iota(I32, x.shape, 1)
    return jnp.where(jnp.right_shift(lane, 6) == g, x, 0.0)


_NT = (((1,), (1,)), ((), ()))


def _attn_p_kernel(q_ref, qi_ref, wi_ref, k_ref, v_ref, kix_ref, abuf_ref, o_ref, key_ref, neg_ref, c0_ref,
                   *, n_keys, q_base):
    del abuf_ref
    t0 = q_base + pl.program_id(1) * TQ
    qi = qi_ref[...]
    wi = wi_ref[...]
    kix = kix_ref[...]
    lane_blk = jnp.right_shift(lax.broadcasted_iota(I32, (TQ, IDX_WIDTH), 1), 6)
    score = jnp.zeros((TQ, n_keys), F32)
    for h in range(IDX_HEADS):
        x = _head_mask(qi, h)
        x = x + pltpu.roll(x, 2 * IDX_DIM, axis=1)
        x = x + pltpu.roll(x, IDX_DIM, axis=1)
        x_hi, x_lo = _split_hi_lo(x)
        qx = jnp.where((lane_blk & 1) == 0, x_hi, x_lo)
        rel = lax.dot_general(qx, kix, _NT, preferred_element_type=F32)
        score = score + wi[:, h:h + 1] * jnp.maximum(rel, 0.0)
    s_idx = lax.broadcasted_iota(I32, (TQ, n_keys), 1)
    t_idx = t0 + lax.broadcasted_iota(I32, (TQ, n_keys), 0)
    key_ref[...] = _sort_key(score, s_idx <= t_idx)
    sel = _select_topk(key_ref, c0_ref, s_idx, n_keys)
    neg_ref[...] = jnp.where(sel, 0.0, -jnp.inf)

    kb = k_ref[...]
    vb = v_ref[...]
    for hq in range(2):
        qh = q_ref[:, hq * KV_WIDTH:(hq + 1) * KV_WIDTH]
        acc = jnp.zeros((TQ, KV_WIDTH), F32)
        for g in range(KV_HEADS):
            qg = (_head_mask(qh, g) * HEAD_DIM ** -0.5).astype(BF16)
            lg = lax.dot_general(qg, kb, _NT, preferred_element_type=F32) + neg_ref[...]
            p = jnp.exp(lg - jnp.max(lg, axis=1, keepdims=True))
            inv = 1.0 / jnp.sum(p, axis=1, keepdims=True)
            pv = jnp.dot(p.astype(BF16), vb, preferred_element_type=F32)
            acc = acc + _head_mask(pv * inv, g)
        o_ref[:, hq * KV_WIDTH:(hq + 1) * KV_WIDTH] = acc


def _attn_p_group(a_buf, q, qi, wi, kb, vb, kix, q_lo, q_hi, n_keys):
    nq = (q_hi - q_lo) // TQ
    qrow = lambda w_: pl.BlockSpec((TQ, w_), lambda b, i: ((b * SEQ + q_lo) // TQ + i, 0))
    krow = pl.BlockSpec((None, n_keys, KV_WIDTH), lambda b, i: (b, 0, 0))
    return pl.pallas_call(
        functools.partial(_attn_p_kernel, n_keys=n_keys, q_base=q_lo),
        grid=(BATCH, nq),
        in_specs=[qrow(ATTN_WIDTH), qrow(IDX_WIDTH), qrow(LANES), krow, krow, krow,
                  pl.BlockSpec(memory_space=pl.ANY)],
        out_specs=qrow(ATTN_WIDTH),
        out_shape=jax.ShapeDtypeStruct(a_buf.shape, F32),
        input_output_aliases={6: 0},
        scratch_shapes=[pltpu.VMEM((TQ, n_keys), I32), pltpu.VMEM((TQ, n_keys), F32), pltpu.VMEM((TQ, LANES), I32)],
        compiler_params=_cparams(2),
        name=f"attn_p{n_keys}",
    )(q, qi, wi, kb, vb, kix, a_buf)


P_GROUP = 512
_P_GROUPS = tuple((lo, lo + P_GROUP, lo + P_GROUP) for lo in range(0, SEQ, P_GROUP))


def _attn_prompt(a_buf, q, qi, wi, kb, vb, kix):
    seq3 = lambda a_: a_[:N_P].reshape(BATCH, SEQ, KV_WIDTH)
    kb, vb, kix = seq3(kb), seq3(vb), seq3(kix)
    for lo, hi, nk in _P_GROUPS:
        a_buf = _attn_p_group(a_buf, q, qi, wi, kb, vb, kix, lo, hi, nk)
    return a_buf


S_KEYS = PAST_LEN + LANES


S_PAIR = 2
S_ROWS = S_PAIR * DEC_SEQ


def _attn_s_kernel(pt_ref, q_ref, qi_ref, wi_ref, kn_ref, vn_ref, kin_ref, *rest):
    n_pg = S_PAIR * N_PAGES
    kT, vT, kiT = rest[0:n_pg], rest[n_pg:2 * n_pg], rest[2 * n_pg:3 * n_pg]
    o_ref, kTs, vTs, kiTs, key_ref, c0_ref = rest[3 * n_pg:]
    del pt_ref
    pad = jnp.zeros((LANES - DEC_SEQ, KV_WIDTH), F32)
    s_idx = lax.broadcasted_iota(I32, (DEC_SEQ, S_KEYS), 1)
    t_idx = PAST_LEN + lax.broadcasted_iota(I32, (DEC_SEQ, S_KEYS), 0)
    new_kv = []
    for s in range(S_PAIR):
        rows = slice(s * DEC_SEQ, (s + 1) * DEC_SEQ)
        for j in range(N_PAGES):
            sl = slice(j * PAGE_SIZE, (j + 1) * PAGE_SIZE)
            kTs[s, :, sl] = kT[s * N_PAGES + j][...].astype(BF16)
            vTs[s, :, sl] = vT[s * N_PAGES + j][...].astype(BF16)
            kiTs[s, :, sl] = kiT[s * N_PAGES + j][...].astype(BF16)
        kn = jnp.concatenate([kn_ref[rows, :], pad], axis=0).astype(BF16)
        vn = jnp.concatenate([vn_ref[rows, :], pad], axis=0).astype(BF16)
        kin = jnp.concatenate([kin_ref[rows, :], pad], axis=0)[:, :IDX_DIM].astype(BF16)
        new_kv.append((kn, vn))

        qi = qi_ref[rows, :]
        wi = wi_ref[rows, :]
        qis = jnp.concatenate([qi[:, h * IDX_DIM:(h + 1) * IDX_DIM] for h in range(IDX_HEADS)], axis=0).astype(BF16)
        rel = jnp.concatenate([jnp.dot(qis, kiTs[s], preferred_element_type=F32),
                               lax.dot_general(qis, kin, _NT, preferred_element_type=F32)], axis=1)
        rel = jnp.maximum(rel, 0.0)
        score = jnp.zeros((DEC_SEQ, S_KEYS), F32)
        for h in range(IDX_HEADS):
            score = score + wi[:, h:h + 1] * rel[h * DEC_SEQ:(h + 1) * DEC_SEQ]
        key_ref[rows, :] = _sort_key(score, s_idx <= t_idx)

    s_idx_all = lax.broadcasted_iota(I32, (S_ROWS, S_KEYS), 1)
    sel = _select_topk(key_ref, c0_ref, s_idx_all, S_KEYS, two_bits=True)
    neg_all = jnp.where(sel, 0.0, -jnp.inf)

    for s in range(S_PAIR):
        rows = slice(s * DEC_SEQ, (s + 1) * DEC_SEQ)
        kn, vn = new_kv[s]
        neg = neg_all[rows]
        q = q_ref[rows, :]
        qs = jnp.concatenate([_head_mask(q[:, hq * KV_WIDTH:(hq + 1) * KV_WIDTH], g)
                              for hq in range(2) for g in range(KV_HEADS)], axis=0).astype(BF16)
        lg = jnp.concatenate([jnp.dot(qs, kTs[s], preferred_element_type=F32),
                              lax.dot_general(qs, kn, _NT, preferred_element_type=F32)], axis=1)
        lg = lg * HEAD_DIM ** -0.5 + jnp.concatenate([neg] * N_HEADS, axis=0)
        p = jnp.exp(lg - jnp.max(lg, axis=1, keepdims=True))
        inv = 1.0 / jnp.sum(p, axis=1, keepdims=True)
        pb = p.astype(BF16)
        pv = (lax.dot_general(pb[:, :PAST_LEN], vTs[s], _NT, preferred_element_type=F32)
              + jnp.dot(pb[:, PAST_LEN:], vn, preferred_element_type=F32)) * inv
        for hq in range(2):
            acc = jnp.zeros((DEC_SEQ, KV_WIDTH), F32)
            for g in range(KV_HEADS):
                r = (hq * KV_HEADS + g) * DEC_SEQ
                acc = acc + _head_mask(pv[r:r + DEC_SEQ], g)
            o_ref[rows, hq * KV_WIDTH:(hq + 1) * KV_WIDTH] = acc


def _attn_sample(layer, page_table, q, qi, wi, k, v, ki, ckT, cvT, ckiT):
    srow = lambda w_: pl.BlockSpec((S_ROWS, w_), lambda b, pt: (N_P // S_ROWS + b, 0))

    def page(rows, s, j):
        return pl.BlockSpec((None, None, rows, PAGE_SIZE),
                            lambda b, pt: (layer, pt[(b * S_PAIR + s) * N_PAGES + j], 0, 0))

    pages = lambda rows: [page(rows, s, j) for s in range(S_PAIR) for j in range(N_PAGES)]
    in_specs = [srow(ATTN_WIDTH), srow(IDX_WIDTH), srow(LANES), srow(KV_WIDTH), srow(KV_WIDTH), srow(KV_WIDTH)]
    in_specs += pages(KV_WIDTH) + pages(KV_WIDTH) + pages(IDX_DIM)
    n_pg = S_PAIR * N_PAGES
    grid_spec = pltpu.PrefetchScalarGridSpec(
        num_scalar_prefetch=1,
        grid=(DEC_BATCH // S_PAIR,),
        in_specs=in_specs,
        out_specs=srow(ATTN_WIDTH),
        scratch_shapes=[pltpu.VMEM((S_PAIR, KV_WIDTH, PAST_LEN), BF16), pltpu.VMEM((S_PAIR, KV_WIDTH, PAST_LEN), BF16),
                        pltpu.VMEM((S_PAIR, IDX_DIM, PAST_LEN), BF16), pltpu.VMEM((S_ROWS, S_KEYS), I32),
                        pltpu.VMEM((S_ROWS, LANES), I32)],
    )
    return pl.pallas_call(
        _attn_s_kernel,
        grid_spec=grid_spec,
        out_shape=jax.ShapeDtypeStruct((N_TOK, ATTN_WIDTH), F32),
        compiler_params=_cparams(1),
        name="attn_s",
    )(page_table, q, qi, wi, k, v, ki, *([ckT] * n_pg), *([cvT] * n_pg), *([ckiT] * n_pg))


def _layer_norm(y, g, b):
    mu = jnp.mean(y, axis=1, keepdims=True)
    d = y - mu
    var = jnp.mean(d * d, axis=1, keepdims=True)
    return d * lax.rsqrt(var + LN_EPS) * g + b


def _merge_kernel(x_ref, a_ref, yc_ref, wg_ref, wpa_ref, wpb_ref, wo_ref,
                  g_ref, b_ref, wr_ref, wrl_ref, br_ref, x1_ref, te_ref, tg_ref):
    x = x_ref[...]
    gates = jax.nn.sigmoid(jnp.dot(x.astype(BF16), wg_ref[...], preferred_element_type=F32))
    pa = jnp.dot(a_ref[...].astype(BF16), wpa_ref[...], preferred_element_type=F32)
    pb = jnp.dot(yc_ref[...].astype(BF16), wpb_ref[...], preferred_element_type=F32)
    merged = gates[:, :D_MODEL] * pa + gates[:, D_MODEL:] * pb
    mix = jnp.dot(merged.astype(BF16), wo_ref[...], preferred_element_type=F32)
    x1 = _layer_norm(DEEPNORM_ALPHA * x + mix, g_ref[...], b_ref[...])
    x1_ref[...] = x1

    x1h, x1l = _split_hi_lo(x1)
    wrh = wr_ref[...]
    wrl = wrl_ref[...]
    lg = (jnp.dot(x1h, wrh, preferred_element_type=F32) + jnp.dot(x1l, wrh, preferred_element_type=F32)
          + jnp.dot(x1h, wrl, preferred_element_type=F32) + jnp.dot(x1l, wrl, preferred_element_type=F32)
          + br_ref[...])
    lane = lax.broadcasted_iota(I32, lg.shape, 1)
    te = jnp.zeros(lg.shape, I32)
    vals = []
    for r in range(TOP_K):
        m = jnp.max(lg, axis=1, keepdims=True)
        idx = jnp.min(jnp.where(lg == m, lane, LANES), axis=1, keepdims=True)
        te = jnp.where(lane == r, idx, te)
        vals.append(m)
        lg = jnp.where(lane == idx, -jnp.inf, lg)
    ex = [jnp.exp(v_ - vals[0]) for v_ in vals]
    inv = 1.0 / (ex[0] + ex[1] + ex[2] + ex[3])
    tg = jnp.zeros(lg.shape, F32)
    for r in range(TOP_K):
        tg = jnp.where(lane == r, ex[r] * inv, tg)
    te_ref[...] = te
    tg_ref[...] = tg


def _merge(x, a, yc, wg, wpa, wpb, wo, g, b, wr, wrl, br):
    n = x.shape[0]
    row = lambda w_: pl.BlockSpec((TM, w_), lambda i: (i, 0))
    return pl.pallas_call(
        _merge_kernel,
        grid=(n // TM,),
        in_specs=[row(D_MODEL), row(ATTN_WIDTH), row(CONV_DIM),
                  _const_spec(wg.shape), _const_spec(wpa.shape), _const_spec(wpb.shape), _const_spec(wo.shape),
                  _const_spec(g.shape), _const_spec(b.shape),
                  _const_spec(wr.shape), _const_spec(wrl.shape), _const_spec(br.shape)],
        out_specs=[row(D_MODEL), row(LANES), row(LANES)],
        out_shape=[jax.ShapeDtypeStruct((n, D_MODEL), F32),
                   jax.ShapeDtypeStruct((n, LANES), I32), jax.ShapeDtypeStruct((n, LANES), F32)],
        compiler_params=_cparams(1),
        name="merge",
    )(x, a, yc, wg, wpa, wpb, wo, g, b, wr, wrl, br)


PAIR_W = 2 * LANES


def _moe_kernel(be_ref, na_ref, nv_ref, xs_ref, w1_ref, b1g_ref, b1l_ref, w2_ref, b2_ref, sel_ref, o_ref,
                w1g_s, w1l_s, w2_s):
    i = pl.program_id(0)
    active = i < na_ref[0]
    new_expert = jnp.logical_or(i == 0, be_ref[i] != be_ref[jnp.maximum(i - 1, 0)])

    @pl.when(jnp.logical_and(active, new_expert))
    def _():
        sel = sel_ref[...]
        for c in range(2 * D_EXPERT // PAIR_W):
            wc = w1_ref[:, c * PAIR_W:(c + 1) * PAIR_W].astype(BF16)
            d = jnp.dot(wc, sel, preferred_element_type=F32).astype(BF16)
            w1g_s[:, c * LANES:(c + 1) * LANES] = d[:, :LANES]
            w1l_s[:, c * LANES:(c + 1) * LANES] = d[:, LANES:]
        w2_s[...] = w2_ref[...].astype(BF16)

    @pl.when(active)
    def _():
        row = lax.broadcasted_iota(I32, (MOE_BM, D_MODEL), 0)
        xs = jnp.where(row < nv_ref[i], xs_ref[...], 0.0).astype(BF16)
        hg = jnp.dot(xs, w1g_s[...], preferred_element_type=F32) + b1g_ref[...]
        hl = jnp.dot(xs, w1l_s[...], preferred_element_type=F32) + b1l_ref[...]
        glu = jnp.minimum(hg, SWIGLU_LIMIT)
        lin = jnp.clip(hl, -SWIGLU_LIMIT, SWIGLU_LIMIT)
        act = glu * jax.nn.sigmoid(SWIGLU_ALPHA * glu) * (lin + 1.0)
        o_ref[...] = jnp.dot(act.astype(BF16), w2_s[...], preferred_element_type=F32) + b2_ref[...]

    @pl.when(jnp.logical_not(active))
    def _():
        o_ref[...] = jnp.zeros(o_ref.shape, F32)


def _pair_select():
    r = np.arange(PAIR_W)[:, None]
    c = np.arange(PAIR_W)[None, :]
    src = np.where(c < LANES, 2 * c, 2 * (c - LANES) + 1)
    return jnp.asarray(r == src, BF16)


def _moe(layer, blk_e, n_act, n_valid, xs, w1, b1g, b1l, w2, b2):
    wspec = lambda r, c: pl.BlockSpec((None, None, r, c), lambda i, be, na, nv: (layer, be[i], 0, 0))
    grid_spec = pltpu.PrefetchScalarGridSpec(
        num_scalar_prefetch=3,
        grid=(MOE_NBLK,),
        in_specs=[pl.BlockSpec((MOE_BM, D_MODEL), lambda i, be, na, nv: (i, 0)),
                  wspec(D_MODEL, 2 * D_EXPERT), wspec(1, D_EXPERT), wspec(1, D_EXPERT),
                  wspec(D_EXPERT, D_MODEL), wspec(1, D_MODEL),
                  pl.BlockSpec((PAIR_W, PAIR_W), lambda i, be, na, nv: (0, 0))],
        out_specs=pl.BlockSpec((MOE_BM, D_MODEL), lambda i, be, na, nv: (i, 0)),
        scratch_shapes=[pltpu.VMEM((D_MODEL, D_EXPERT), BF16), pltpu.VMEM((D_MODEL, D_EXPERT), BF16),
                        pltpu.VMEM((D_EXPERT, D_MODEL), BF16)],
    )
    return pl.pallas_call(
        _moe_kernel,
        grid_spec=grid_spec,
        out_shape=jax.ShapeDtypeStruct((MOE_ROWS, D_MODEL), F32),
        compiler_params=_cparams(1),
        name="moe",
    )(blk_e, n_act, n_valid, xs, w1, b1g, b1l, w2, b2, _pair_select())


SC_ROWS = 64


def _sc_gather(table, idx):
    n_idx = idx.shape[0]
    width = table.shape[1]
    info = plsc.get_sparse_core_info()
    n_cores = info.num_cores
    n_workers = n_cores * info.num_subcores
    per_worker = n_idx // n_workers
    assert per_worker * n_workers == n_idx and per_worker % SC_ROWS == 0
    mesh = plsc.VectorSubcoreMesh(core_axis_name="c", subcore_axis_name="s")

    @functools.partial(
        pl.kernel, mesh=mesh,
        out_type=jax.ShapeDtypeStruct((n_idx, width), table.dtype),
        scratch_types=[pltpu.VMEM((SC_ROWS,), I32), pltpu.VMEM((SC_ROWS, width), table.dtype),
                       pltpu.SemaphoreType.DMA],
    )
    def gather_kernel(table_hbm, idx_hbm, out_hbm, idx_v, rows_v, sem):
        base = (lax.axis_index("s") * n_cores + lax.axis_index("c")) * per_worker

        @pl.loop(0, per_worker // SC_ROWS)
        def _(j):
            off = base + j * SC_ROWS
            pltpu.sync_copy(idx_hbm.at[pl.ds(off, SC_ROWS)], idx_v)
            pltpu.async_copy(table_hbm.at[idx_v], rows_v, sem).wait()
            pltpu.sync_copy(rows_v, out_hbm.at[pl.ds(off, SC_ROWS)])

    return gather_kernel(table, idx)


SC_TOKENS = 32


def _sc_scatter_rows(x, dest_t, n_rows):
    n_tok, width = x.shape
    top_k = dest_t.shape[0]
    info = plsc.get_sparse_core_info()
    n_cores = info.num_cores
    n_workers = n_cores * info.num_subcores
    per_worker = n_tok // n_workers
    assert per_worker * n_workers == n_tok and per_worker % SC_TOKENS == 0
    mesh = plsc.VectorSubcoreMesh(core_axis_name="c", subcore_axis_name="s")

    @functools.partial(
        pl.kernel, mesh=mesh,
        out_type=jax.ShapeDtypeStruct((n_rows, width), x.dtype),
        scratch_types=[pltpu.VMEM((SC_TOKENS,), I32), pltpu.VMEM((SC_TOKENS, width), x.dtype)],
    )
    def scatter_kernel(x_hbm, idx_hbm, out_hbm, idx_v, rows_v):
        base = (lax.axis_index("s") * n_cores + lax.axis_index("c")) * per_worker

        @pl.loop(0, per_worker // SC_TOKENS)
        def _(j):
            t0 = base + j * SC_TOKENS
            pltpu.sync_copy(x_hbm.at[pl.ds(t0, SC_TOKENS)], rows_v)
            for k in range(top_k):
                pltpu.sync_copy(idx_hbm.at[pl.ds(k * n_tok + t0, SC_TOKENS)], idx_v)
                pltpu.sync_copy(rows_v, out_hbm.at[idx_v])

    return scatter_kernel(x, dest_t.reshape(-1))


def _route(top_e):
    flat_e = top_e.reshape(-1)
    onehot = (flat_e[:, None] == jnp.arange(N_EXPERTS, dtype=I32)[None, :]).astype(I32)
    csum = jnp.cumsum(onehot, axis=0)
    counts = csum[-1]
    rank = jnp.take_along_axis(csum, flat_e[:, None], axis=1)[:, 0] - 1
    padded = (counts + MOE_BM - 1) // MOE_BM * MOE_BM
    pad_end = jnp.cumsum(padded)
    pad_start = pad_end - padded
    dest_t = (pad_start[flat_e] + rank).reshape(N_TOK, TOP_K).T
    blk_start = jnp.arange(MOE_NBLK, dtype=I32) * MOE_BM
    blk_e = jnp.minimum(jnp.sum(blk_start[:, None] >= pad_end[None, :], axis=1), N_EXPERTS - 1).astype(I32)
    n_act = (pad_end[-1] // MOE_BM).astype(I32).reshape(1)
    n_valid = jnp.clip((pad_start + counts)[blk_e] - blk_start, 0, MOE_BM).astype(I32)
    return dest_t, blk_e, n_act, n_valid


def _ln2_kernel(x_ref, y_ref, tg_ref, g_ref, b_ref, o_ref):
    tg = tg_ref[...]
    y = tg[:, 0:1] * y_ref[0]
    for r in range(1, TOP_K):
        y = y + tg[:, r:r + 1] * y_ref[r]
    o_ref[...] = _layer_norm(DEEPNORM_ALPHA * x_ref[...] + y, g_ref[...], b_ref[...])


def _ln2(x1, yk, tg, g, b):
    n = x1.shape[0]
    return pl.pallas_call(
        _ln2_kernel,
        grid=(n // TM,),
        in_specs=[pl.BlockSpec((TM, D_MODEL), lambda i: (i, 0)),
                  pl.BlockSpec((TOP_K, TM, D_MODEL), lambda i: (0, i, 0)),
                  pl.BlockSpec((TM, LANES), lambda i: (i, 0)),
                  _const_spec(g.shape), _const_spec(b.shape)],
        out_specs=pl.BlockSpec((TM, D_MODEL), lambda i: (i, 0)),
        out_shape=jax.ShapeDtypeStruct((n, D_MODEL), F32),
        compiler_params=_cparams(1),
        name="ln2",
    )(x1, yk, tg, g, b)


def _rot_cols(w, n_heads):
    lead = w.shape[:-1]
    w4 = w.reshape(*lead, n_heads, 2, HEAD_DIM // 2)
    return jnp.concatenate([-w4[..., 1, :], w4[..., 0, :]], axis=-1).reshape(*lead, n_heads * HEAD_DIM)


def _pair_major(w):
    lead = w.shape[:-1]
    w4 = w.reshape(*lead, KV_HEADS, 2, HEAD_DIM)
    return jnp.swapaxes(w4, -3, -2).reshape(*lead, ATTN_WIDTH)


def _prep_weights(w_in, w_pa, w_router, b_router):
    c = np.cumsum((ATTN_WIDTH, KV_WIDTH, KV_WIDTH, IDX_WIDTH, IDX_DIM, IDX_HEADS,
                   CONV_DIM, CONV_DIM, CONV_DIM, D_MODEL, D_MODEL)).tolist()
    wq, wk, wv = w_in[..., :c[0]], w_in[..., c[0]:c[1]], w_in[..., c[1]:c[2]]
    wqi, wki, wwi = w_in[..., c[2]:c[3]], w_in[..., c[3]:c[4]], w_in[..., c[4]:c[5]]
    wconv = w_in[..., c[5]:c[8]]
    wgate = w_in[..., c[8]:]
    wki_r = _rot_cols(wki, 1)
    wwi = jnp.pad(wwi * (IDX_HEADS ** -0.5 * IDX_DIM ** -0.5), ((0, 0), (0, 0), (0, LANES - IDX_HEADS)))
    widx = jnp.concatenate([wqi, _rot_cols(wqi, IDX_HEADS), jnp.tile(wki, (1, 1, IDX_HEADS)),
                            jnp.tile(wki_r, (1, 1, IDX_HEADS)), wwi], axis=-1)
    widx_hi, w1lo = _split_hi_lo(widx)
    w1cat = jnp.concatenate(
        [_pair_major(wq).astype(BF16), _pair_major(_rot_cols(wq, N_HEADS)).astype(BF16), wk.astype(BF16),
         _rot_cols(wk, KV_HEADS).astype(BF16), wv.astype(BF16), widx_hi, wconv.astype(BF16)], axis=-1)
    wpa = jnp.swapaxes(w_pa.reshape(DEPTH, KV_HEADS, 2, HEAD_DIM, D_MODEL), 1, 2).reshape(DEPTH, ATTN_WIDTH, D_MODEL)
    wr, wrl = _split_hi_lo(jnp.pad(w_router, ((0, 0), (0, 0), (0, LANES - N_EXPERTS))))
    br = jnp.pad(b_router, ((0, 0), (0, LANES - N_EXPERTS)), constant_values=-jnp.inf)[:, None, :]
    return w1cat, w1lo, wgate.astype(BF16), wpa.astype(BF16), wr, wrl, br


def _rope_tables():
    pos = jnp.concatenate([jnp.tile(jnp.arange(SEQ, dtype=I32), BATCH),
                           jnp.tile(PAST_LEN + jnp.arange(DEC_SEQ, dtype=I32), DEC_BATCH)])
    half = HEAD_DIM // 2
    inv = ROPE_THETA ** (-jnp.arange(half, dtype=F32) / half)
    ang = pos.astype(F32)[:, None] * inv[None, :]
    return jnp.tile(jnp.cos(ang), (1, LANES // half)), jnp.tile(jnp.sin(ang), (1, LANES // half))


def kernel(x_prompt, x_sample, cache_k, cache_v, cache_kidx, state_conv, page_table, w_in, w_conv, w_pa, w_pb,
           w_o, ln1_g, ln1_b, w_router, b_router, w1, b1, w2, b2, ln2_g, ln2_b):
    w1cat, w1lo, wgate, wpa, wr, wrl, br = _prep_weights(w_in, w_pa, w_router, b_router)
    wpb = w_pb.astype(BF16)
    wo = w_o.astype(BF16)
    wc = jnp.pad(w_conv, ((0, 0), (0, 8 - w_conv.shape[1]), (0, 0)))
    b1g = b1[..., 0::2][:, :, None, :]
    b1l = b1[..., 1::2][:, :, None, :]
    b2r = b2[:, :, None, :]
    cos, sin = _rope_tables()
    zrows = jnp.zeros((DEPTH, DEC_BATCH, DEC_SEQ - 2, CONV_DIM), F32)
    h1 = jnp.concatenate([state_conv[:, :, 1:2], zrows[:, :, :1], zrows], axis=2).reshape(DEPTH, N_S, CONV_DIM)
    h2 = jnp.concatenate([state_conv, zrows], axis=2).reshape(DEPTH, N_S, CONV_DIM)
    ckT = jnp.transpose(cache_k, (0, 1, 3, 4, 2)).reshape(DEPTH, -1, KV_WIDTH, PAGE_SIZE)
    cvT = jnp.transpose(cache_v, (0, 1, 3, 4, 2)).reshape(DEPTH, -1, KV_WIDTH, PAGE_SIZE)
    ckiT = jnp.transpose(cache_kidx, (0, 1, 3, 2))
    pt = page_table.reshape(-1).astype(I32)

    x = jnp.concatenate([x_prompt.reshape(N_P, D_MODEL), x_sample.reshape(N_S, D_MODEL)], axis=0)
    ks, vs, kis, convs = [], [], [], []
    for l in range(DEPTH):
        q, qi, wi, k, v, ki4, kb, vb, kix, yc, u = _inproj(x, w1cat[l], w1lo[l], cos, sin, wc[l], h1[l], h2[l])
        a = _attn_sample(l, pt, q, qi, wi, k, v, ki4, ckT, cvT, ckiT)
        a = _attn_prompt(a, q, qi, wi, kb, vb, kix)

        x1, te, tg = _merge(x, a, yc, wgate[l], wpa[l], wpb[l], wo[l],
                            ln1_g[l][None], ln1_b[l][None], wr[l], wrl[l], br[l])
        dest_t, blk_e, n_act, n_valid = _route(te[:, :TOP_K])
        xs = _sc_scatter_rows(x1, dest_t, MOE_ROWS)
        y_rows = _moe(l, blk_e, n_act, n_valid, xs, w1, b1g, b1l, w2, b2r)
        yk = _sc_gather(y_rows, dest_t.reshape(-1)).reshape(TOP_K, N_TOK, D_MODEL)
        x = _ln2(x1, yk, tg, ln2_g[l][None], ln2_b[l][None])

        ks.append(k)
        vs.append(v)
        kis.append(ki4[:, :IDX_DIM])
        convs.append((u[:N_P].reshape(BATCH, SEQ, CONV_DIM)[:, -2:],
                      u[N_P:].reshape(DEC_BATCH, DEC_SEQ, CONV_DIM)[:, -2:]))

    def split(parts, tail):
        st = jnp.stack(parts, axis=0)
        return (st[:, :N_P].reshape(DEPTH, BATCH, SEQ, *tail), st[:, N_P:].reshape(DEPTH, DEC_BATCH, DEC_SEQ, *tail))

    k_p, k_s = split(ks, (KV_HEADS, HEAD_DIM))
    v_p, v_s = split(vs, (KV_HEADS, HEAD_DIM))
    ki_p, ki_s = split(kis, (IDX_DIM,))
    conv_p = jnp.stack([c_[0] for c_ in convs], axis=0)
    conv_s = jnp.stack([c_[1] for c_ in convs], axis=0)
    y_p = x[:N_P].reshape(BATCH, SEQ, D_MODEL)
    y_s = x[N_P:].reshape(DEC_BATCH, DEC_SEQ, D_MODEL)
    return (y_p, y_s, k_p, v_p, ki_p, conv_p, k_s, v_s, ki_s, conv_s)
```

```python
import functools

import numpy as np
import jax
import jax.numpy as jnp
from jax import lax
from jax.experimental import pallas as pl
from jax.experimental.pallas import tpu as pltpu
from jax.experimental.pallas import tpu_sc as plsc

F32 = jnp.float32
BF16 = jnp.bfloat16
I32 = jnp.int32

D_MODEL = 1024
BATCH = 4
SEQ = 4096
DEPTH = 4
DEC_BATCH = 128
DEC_SEQ = 8
PAGE_SIZE = 128
N_PAGES = 16
PAST_LEN = N_PAGES * PAGE_SIZE
N_HEADS = 8
HEAD_DIM = 64
KV_HEADS = 4
IDX_HEADS = 4
IDX_DIM = 64
TOPK = 256
CONV_DIM = 512
N_EXPERTS = 32
TOP_K = 4
D_EXPERT = 1024
SWIGLU_LIMIT = 7.0
SWIGLU_ALPHA = 1.702
ROPE_THETA = 10000.0
LN_EPS = 1e-5
ATTN_WIDTH = N_HEADS * HEAD_DIM
KV_WIDTH = KV_HEADS * HEAD_DIM
IDX_WIDTH = IDX_HEADS * IDX_DIM
DEEPNORM_ALPHA = (2 * DEPTH) ** 0.25

N_P = BATCH * SEQ
N_S = DEC_BATCH * DEC_SEQ
N_TOK = N_P + N_S

LANES = 128
VMEM_LIMIT = 56 * 1024 * 1024
INT_MIN = -2 ** 31

_SEC = {}
_off = 0
for _name, _w in (("q", 512), ("qr", 512), ("k", 256), ("kr", 256), ("v", 256), ("qi", 256), ("qir", 256),
                  ("ki", 256), ("kir", 256), ("wi", 128), ("cb", 512), ("cc", 512), ("cx", 512)):
    _SEC[_name] = (_off, _off + _w)
    _off += _w
W1_COLS = _off
_SEC_LO = {}
_off = 0
for _name in ("qi", "qir", "ki", "kir", "wi"):
    _w = _SEC[_name][1] - _SEC[_name][0]
    _SEC_LO[_name] = (_off, _off + _w)
    _off += _w
W1_LO_COLS = _off

TM = 512
TQ = 256
MOE_BM = 256
MOE_ROWS = N_TOK * TOP_K + N_EXPERTS * MOE_BM
MOE_NBLK = MOE_ROWS // MOE_BM


def _cparams(n_axes):
    return pltpu.CompilerParams(dimension_semantics=("arbitrary",) * n_axes, vmem_limit_bytes=VMEM_LIMIT)


def _const_spec(shape):
    nd = len(shape)
    return pl.BlockSpec(shape, lambda *_: (0,) * nd)


def _split_hi_lo(x):
    hi = lax.bitcast_convert_type(lax.bitcast_convert_type(x, I32) & jnp.int32(-65536), F32)
    return hi.astype(BF16), (x - hi).astype(BF16)


def _inproj_kernel(x_ref, w_ref, wlo_ref, cos_ref, sin_ref, wc_ref, h1_ref, h2_ref,
                   q_ref, qi_ref, wi_ref, k_ref, v_ref, ki_ref, kb_ref, vb_ref, kix_ref, yc_ref, u_ref, tail_ref):
    x = x_ref[...]
    xb = x.astype(BF16)
    xhi, xlo = _split_hi_lo(x)

    def mm(name):
        a, b = _SEC[name]
        if name not in _SEC_LO:
            return jnp.dot(xb, w_ref[:, a:b], preferred_element_type=F32)
        la, lb = _SEC_LO[name]
        return (jnp.dot(xhi, w_ref[:, a:b], preferred_element_type=F32)
                + jnp.dot(xlo, w_ref[:, a:b], preferred_element_type=F32)
                + jnp.dot(xhi, wlo_ref[:, la:lb], preferred_element_type=F32)
                + jnp.dot(xlo, wlo_ref[:, la:lb], preferred_element_type=F32))

    cos = cos_ref[...]
    sin = sin_ref[...]
    cos2 = jnp.concatenate([cos, cos], axis=1)
    sin2 = jnp.concatenate([sin, sin], axis=1)
    cos4 = jnp.concatenate([cos2, cos2], axis=1)
    sin4 = jnp.concatenate([sin2, sin2], axis=1)
    q_ref[...] = mm("q") * cos4 + mm("qr") * sin4
    qi_ref[...] = mm("qi") * cos2 + mm("qir") * sin2
    k = mm("k") * cos2 + mm("kr") * sin2
    k_ref[...] = k
    kb_ref[...] = k.astype(BF16)
    ki = mm("ki") * cos2 + mm("kir") * sin2
    ki_ref[...] = ki
    ki_hi, ki_lo = _split_hi_lo(ki)
    kix_ref[...] = jnp.concatenate([ki_hi[:, :LANES], ki_lo[:, :LANES]], axis=1)
    v = mm("v")
    v_ref[...] = v
    vb_ref[...] = v.astype(BF16)
    wi_ref[...] = mm("wi")

    i = pl.program_id(0)
    u = mm("cc") * mm("cx")
    u_ref[...] = u

    @pl.when(i == 0)
    def _():
        tail_ref[...] = jnp.zeros(tail_ref.shape, F32)

    row = lax.broadcasted_iota(I32, (TM, CONV_DIM), 0)
    back1 = pltpu.roll(u, 1, axis=0)
    back2 = pltpu.roll(u, 2, axis=0)
    mid_seq = lax.rem(i * TM, SEQ) != 0
    tail = jnp.where(mid_seq, tail_ref[...], 0.0)
    p1 = jnp.where(row == 0, tail[7:8], back1)
    p2 = jnp.where(row == 0, tail[6:7], jnp.where(row == 1, tail[7:8], back2))
    pos = lax.rem(row, DEC_SEQ)
    s1 = jnp.where(pos == 0, h1_ref[...], back1)
    s2 = jnp.where(pos < 2, h2_ref[...], back2)
    is_sample = i >= N_P // TM
    um1 = jnp.where(is_sample, s1, p1)
    um2 = jnp.where(is_sample, s2, p2)
    tail_ref[...] = u[TM - 8:, :]
    wc = wc_ref[...]
    yc_ref[...] = mm("cb") * (wc[0:1] * um2 + wc[1:2] * um1 + wc[2:3] * u)


def _inproj(x, w, wlo, cos, sin, wc, h1, h2):
    n = x.shape[0]
    row = lambda w_: pl.BlockSpec((TM, w_), lambda i: (i, 0))
    srow = pl.BlockSpec((TM, CONV_DIM), lambda i: (jnp.maximum(i - N_P // TM, 0), 0))
    widths = (512, 256, 128, 256, 256, 256, 256, 256, 256, 512, 512)
    dtypes = (F32, F32, F32, F32, F32, F32, BF16, BF16, BF16, F32, F32)
    return pl.pallas_call(
        _inproj_kernel,
        grid=(n // TM,),
        in_specs=[row(D_MODEL), _const_spec((D_MODEL, W1_COLS)), _const_spec((D_MODEL, W1_LO_COLS)),
                  row(LANES), row(LANES), _const_spec(wc.shape), srow, srow],
        out_specs=[row(w_) for w_ in widths],
        out_shape=[jax.ShapeDtypeStruct((n, w_), d_) for w_, d_ in zip(widths, dtypes)],
        scratch_shapes=[pltpu.VMEM((8, CONV_DIM), F32)],
        compiler_params=_cparams(1),
        name="inproj",
    )(x, w, wlo, cos, sin, wc, h1, h2)


def _sort_key(score, valid):
    bits = pltpu.bitcast(score, I32)
    key = jnp.where(bits < 0, bits ^ jnp.int32(0x7FFFFFFF), bits)
    key = jnp.where(score == 0.0, jnp.int32(0), key)
    return jnp.where(valid, key, jnp.int32(INT_MIN))


def _count(mask):
    return jnp.sum(jnp.where(mask, 1.0, 0.0), axis=1, keepdims=True)


def _select_topk(key_ref, c0_ref, s_idx, n_keys, two_bits=False):
    rows = key_ref.shape[0]
    kf = float(TOPK)

    def thr_step(it, t):
        cand = t + jnp.left_shift(jnp.int32(1), 31 - it)
        cnt = _count(key_ref[...] >= cand)
        return jnp.where(cnt >= kf, cand, t)

    def thr_step2(it, t):
        d = jnp.left_shift(jnp.int32(1), 30 - 2 * it)
        c1 = t + d
        c2 = c1 + d
        c3 = c2 + d
        key = key_ref[...]
        n1, n2, n3 = _count(key >= c1), _count(key >= c2), _count(key >= c3)
        return jnp.where(n3 >= kf, c3, jnp.where(n2 >= kf, c2, jnp.where(n1 >= kf, c1, t)))

    thr0 = jnp.full((rows, 1), INT_MIN, I32)
    thr = lax.fori_loop(0, 16, thr_step2, thr0) if two_bits else lax.fori_loop(0, 32, thr_step, thr0)
    key = key_ref[...]
    live = key > jnp.int32(INT_MIN)
    n_gt = _count(key > thr)
    n_ge = _count(jnp.logical_and(key >= thr, live))
    c0_ref[...] = jnp.full(c0_ref.shape, n_keys, I32)

    @pl.when(jnp.max(n_ge) > kf)
    def _():
        need = kf - n_gt
        nbits = int(n_keys).bit_length()

        def idx_step(it, c):
            cand = c + jnp.left_shift(jnp.int32(1), nbits - 1 - it)
            eq = jnp.logical_and(key_ref[...] == thr, s_idx < cand)
            return jnp.where(_count(eq) < need, cand, c)

        c_last = lax.fori_loop(0, nbits, idx_step, jnp.zeros((rows, 1), I32))
        c0_ref[...] = jnp.broadcast_to(c_last, c0_ref.shape)

    c0 = c0_ref[:, 0:1]
    tie = jnp.logical_and(key == thr, s_idx <= c0)
    return jnp.logical_and(jnp.logical_or(key > thr, tie), live)


def _head_mask(x, g):
    lane = lax.broadcasted_iota(I32, x.shape, 1)
    return jnp.where(jnp.right_shift(lane, 6) == g, x, 0.0)


_NT = (((1,), (1,)), ((), ()))


def _attn_p_kernel(q_ref, qi_ref, wi_ref, k_ref, v_ref, kix_ref, abuf_ref, o_ref, key_ref, neg_ref, c0_ref,
                   *, n_keys, q_base):
    del abuf_ref
    t0 = q_base + pl.program_id(1) * TQ
    qi = qi_ref[...]
    wi = wi_ref[...]
    kix = kix_ref[...]
    lane_blk = jnp.right_shift(lax.broadcasted_iota(I32, (TQ, IDX_WIDTH), 1), 6)
    score = jnp.zeros((TQ, n_keys), F32)
    for h in range(IDX_HEADS):
        x = _head_mask(qi, h)
        x = x + pltpu.roll(x, 2 * IDX_DIM, axis=1)
        x = x + pltpu.roll(x, IDX_DIM, axis=1)
        x_hi, x_lo = _split_hi_lo(x)
        qx = jnp.where((lane_blk & 1) == 0, x_hi, x_lo)
        rel = lax.dot_general(qx, kix, _NT, preferred_element_type=F32)
        score = score + wi[:, h:h + 1] * jnp.maximum(rel, 0.0)
    s_idx = lax.broadcasted_iota(I32, (TQ, n_keys), 1)
    t_idx = t0 + lax.broadcasted_iota(I32, (TQ, n_keys), 0)
    key_ref[...] = _sort_key(score, s_idx <= t_idx)
    sel = _select_topk(key_ref, c0_ref, s_idx, n_keys)
    neg_ref[...] = jnp.where(sel, 0.0, -jnp.inf)

    kb = k_ref[...]
    vb = v_ref[...]
    for hq in range(2):
        qh = q_ref[:, hq * KV_WIDTH:(hq + 1) * KV_WIDTH]
        acc = jnp.zeros((TQ, KV_WIDTH), F32)
        for g in range(KV_HEADS):
            qg = (_head_mask(qh, g) * HEAD_DIM ** -0.5).astype(BF16)
            lg = lax.dot_general(qg, kb, _NT, preferred_element_type=F32) + neg_ref[...]
            p = jnp.exp(lg - jnp.max(lg, axis=1, keepdims=True))
            inv = 1.0 / jnp.sum(p, axis=1, keepdims=True)
            pv = jnp.dot(p.astype(BF16), vb, preferred_element_type=F32)
            acc = acc + _head_mask(pv * inv, g)
        o_ref[:, hq * KV_WIDTH:(hq + 1) * KV_WIDTH] = acc


def _attn_p_group(a_buf, q, qi, wi, kb, vb, kix, q_lo, q_hi, n_keys):
    nq = (q_hi - q_lo) // TQ
    qrow = lambda w_: pl.BlockSpec((TQ, w_), lambda b, i: ((b * SEQ + q_lo) // TQ + i, 0))
    krow = pl.BlockSpec((None, n_keys, KV_WIDTH), lambda b, i: (b, 0, 0))
    return pl.pallas_call(
        functools.partial(_attn_p_kernel, n_keys=n_keys, q_base=q_lo),
        grid=(BATCH, nq),
        in_specs=[qrow(ATTN_WIDTH), qrow(IDX_WIDTH), qrow(LANES), krow, krow, krow,
                  pl.BlockSpec(memory_space=pl.ANY)],
        out_specs=qrow(ATTN_WIDTH),
        out_shape=jax.ShapeDtypeStruct(a_buf.shape, F32),
        input_output_aliases={6: 0},
        scratch_shapes=[pltpu.VMEM((TQ, n_keys), I32), pltpu.VMEM((TQ, n_keys), F32), pltpu.VMEM((TQ, LANES), I32)],
        compiler_params=_cparams(2),
        name=f"attn_p{n_keys}",
    )(q, qi, wi, kb, vb, kix, a_buf)


P_GROUP = 256
_P_GROUPS = tuple((lo, lo + P_GROUP, lo + P_GROUP) for lo in range(0, SEQ, P_GROUP))


def _attn_prompt(a_buf, q, qi, wi, kb, vb, kix):
    seq3 = lambda a_: a_[:N_P].reshape(BATCH, SEQ, KV_WIDTH)
    kb, vb, kix = seq3(kb), seq3(vb), seq3(kix)
    for lo, hi, nk in _P_GROUPS:
        a_buf = _attn_p_group(a_buf, q, qi, wi, kb, vb, kix, lo, hi, nk)
    return a_buf


S_KEYS = PAST_LEN + LANES


S_PAIR = 4
S_ROWS = S_PAIR * DEC_SEQ


def _attn_s_kernel(pt_ref, q_ref, qi_ref, wi_ref, kn_ref, vn_ref, kin_ref, *rest):
    n_pg = S_PAIR * N_PAGES
    kT, vT, kiT = rest[0:n_pg], rest[n_pg:2 * n_pg], rest[2 * n_pg:3 * n_pg]
    o_ref, kTs, vTs, kiTs, key_ref, c0_ref = rest[3 * n_pg:]
    del pt_ref
    pad = jnp.zeros((LANES - DEC_SEQ, KV_WIDTH), F32)
    s_idx = lax.broadcasted_iota(I32, (DEC_SEQ, S_KEYS), 1)
    t_idx = PAST_LEN + lax.broadcasted_iota(I32, (DEC_SEQ, S_KEYS), 0)
    new_kv = []
    for s in range(S_PAIR):
        rows = slice(s * DEC_SEQ, (s + 1) * DEC_SEQ)
        for j in range(N_PAGES):
            sl = slice(j * PAGE_SIZE, (j + 1) * PAGE_SIZE)
            kTs[s, :, sl] = kT[s * N_PAGES + j][...].astype(BF16)
            vTs[s, :, sl] = vT[s * N_PAGES + j][...].astype(BF16)
            kiTs[s, :, sl] = kiT[s * N_PAGES + j][...].astype(BF16)
        kn = jnp.concatenate([kn_ref[rows, :], pad], axis=0).astype(BF16)
        vn = jnp.concatenate([vn_ref[rows, :], pad], axis=0).astype(BF16)
        kin = jnp.concatenate([kin_ref[rows, :], pad], axis=0)[:, :IDX_DIM].astype(BF16)
        new_kv.append((kn, vn))

        qi = qi_ref[rows, :]
        wi = wi_ref[rows, :]
        qis = jnp.concatenate([qi[:, h * IDX_DIM:(h + 1) * IDX_DIM] for h in range(IDX_HEADS)], axis=0).astype(BF16)
        rel = jnp.concatenate([jnp.dot(qis, kiTs[s], preferred_element_type=F32),
                               lax.dot_general(qis, kin, _NT, preferred_element_type=F32)], axis=1)
        rel = jnp.maximum(rel, 0.0)
        score = jnp.zeros((DEC_SEQ, S_KEYS), F32)
        for h in range(IDX_HEADS):
            score = score + wi[:, h:h + 1] * rel[h * DEC_SEQ:(h + 1) * DEC_SEQ]
        key_ref[rows, :] = _sort_key(score, s_idx <= t_idx)

    s_idx_all = lax.broadcasted_iota(I32, (S_ROWS, S_KEYS), 1)
    sel = _select_topk(key_ref, c0_ref, s_idx_all, S_KEYS, two_bits=True)
    neg_all = jnp.where(sel, 0.0, -jnp.inf)

    for s in range(S_PAIR):
        rows = slice(s * DEC_SEQ, (s + 1) * DEC_SEQ)
        kn, vn = new_kv[s]
        neg = neg_all[rows]
        q = q_ref[rows, :]
        qs = jnp.concatenate([_head_mask(q[:, hq * KV_WIDTH:(hq + 1) * KV_WIDTH], g)
                              for hq in range(2) for g in range(KV_HEADS)], axis=0).astype(BF16)
        lg = jnp.concatenate([jnp.dot(qs, kTs[s], preferred_element_type=F32),
                              lax.dot_general(qs, kn, _NT, preferred_element_type=F32)], axis=1)
        lg = lg * HEAD_DIM ** -0.5 + jnp.concatenate([neg] * N_HEADS, axis=0)
        p = jnp.exp(lg - jnp.max(lg, axis=1, keepdims=True))
        inv = 1.0 / jnp.sum(p, axis=1, keepdims=True)
        pb = p.astype(BF16)
        pv = (lax.dot_general(pb[:, :PAST_LEN], vTs[s], _NT, preferred_element_type=F32)
              + jnp.dot(pb[:, PAST_LEN:], vn, preferred_element_type=F32)) * inv
        for hq in range(2):
            acc = jnp.zeros((DEC_SEQ, KV_WIDTH), F32)
            for g in range(KV_HEADS):
                r = (hq * KV_HEADS + g) * DEC_SEQ
                acc = acc + _head_mask(pv[r:r + DEC_SEQ], g)
            o_ref[rows, hq * KV_WIDTH:(hq + 1) * KV_WIDTH] = acc


def _attn_sample(layer, page_table, q, qi, wi, k, v, ki, ckT, cvT, ckiT):
    srow = lambda w_: pl.BlockSpec((S_ROWS, w_), lambda b, pt: (N_P // S_ROWS + b, 0))

    def page(rows, s, j):
        return pl.BlockSpec((None, None, rows, PAGE_SIZE),
                            lambda b, pt: (layer, pt[(b * S_PAIR + s) * N_PAGES + j], 0, 0))

    pages = lambda rows: [page(rows, s, j) for s in range(S_PAIR) for j in range(N_PAGES)]
    in_specs = [srow(ATTN_WIDTH), srow(IDX_WIDTH), srow(LANES), srow(KV_WIDTH), srow(KV_WIDTH), srow(KV_WIDTH)]
    in_specs += pages(KV_WIDTH) + pages(KV_WIDTH) + pages(IDX_DIM)
    n_pg = S_PAIR * N_PAGES
    grid_spec = pltpu.PrefetchScalarGridSpec(
        num_scalar_prefetch=1,
        grid=(DEC_BATCH // S_PAIR,),
        in_specs=in_specs,
        out_specs=srow(ATTN_WIDTH),
        scratch_shapes=[pltpu.VMEM((S_PAIR, KV_WIDTH, PAST_LEN), BF16), pltpu.VMEM((S_PAIR, KV_WIDTH, PAST_LEN), BF16),
                        pltpu.VMEM((S_PAIR, IDX_DIM, PAST_LEN), BF16), pltpu.VMEM((S_ROWS, S_KEYS), I32),
                        pltpu.VMEM((S_ROWS, LANES), I32)],
    )
    return pl.pallas_call(
        _attn_s_kernel,
        grid_spec=grid_spec,
        out_shape=jax.ShapeDtypeStruct((N_TOK, ATTN_WIDTH), F32),
        compiler_params=_cparams(1),
        name="attn_s",
    )(page_table, q, qi, wi, k, v, ki, *([ckT] * n_pg), *([cvT] * n_pg), *([ckiT] * n_pg))


def _layer_norm(y, g, b):
    mu = jnp.mean(y, axis=1, keepdims=True)
    d = y - mu
    var = jnp.mean(d * d, axis=1, keepdims=True)
    return d * lax.rsqrt(var + LN_EPS) * g + b


def _merge_kernel(x_ref, a_ref, yc_ref, wg_ref, wpa_ref, wpb_ref, wo_ref,
                  g_ref, b_ref, wr_ref, wrl_ref, br_ref, x1_ref, te_ref, tg_ref):
    x = x_ref[...]
    gates = jax.nn.sigmoid(jnp.dot(x.astype(BF16), wg_ref[...], preferred_element_type=F32))
    pa = jnp.dot(a_ref[...].astype(BF16), wpa_ref[...], preferred_element_type=F32)
    pb = jnp.dot(yc_ref[...].astype(BF16), wpb_ref[...], preferred_element_type=F32)
    merged = gates[:, :D_MODEL] * pa + gates[:, D_MODEL:] * pb
    mix = jnp.dot(merged.astype(BF16), wo_ref[...], preferred_element_type=F32)
    x1 = _layer_norm(DEEPNORM_ALPHA * x + mix, g_ref[...], b_ref[...])
    x1_ref[...] = x1

    x1h, x1l = _split_hi_lo(x1)
    wrh = wr_ref[...]
    wrl = wrl_ref[...]
    lg = (jnp.dot(x1h, wrh, preferred_element_type=F32) + jnp.dot(x1l, wrh, preferred_element_type=F32)
          + jnp.dot(x1h, wrl, preferred_element_type=F32) + jnp.dot(x1l, wrl, preferred_element_type=F32)
          + br_ref[...])
    lane = lax.broadcasted_iota(I32, lg.shape, 1)
    te = jnp.zeros(lg.shape, I32)
    vals = []
    for r in range(TOP_K):
        m = jnp.max(lg, axis=1, keepdims=True)
        idx = jnp.min(jnp.where(lg == m, lane, LANES), axis=1, keepdims=True)
        te = jnp.where(lane == r, idx, te)
        vals.append(m)
        lg = jnp.where(lane == idx, -jnp.inf, lg)
    ex = [jnp.exp(v_ - vals[0]) for v_ in vals]
    inv = 1.0 / (ex[0] + ex[1] + ex[2] + ex[3])
    tg = jnp.zeros(lg.shape, F32)
    for r in range(TOP_K):
        tg = jnp.where(lane == r, ex[r] * inv, tg)
    te_ref[...] = te
    tg_ref[...] = tg


def _merge(x, a, yc, wg, wpa, wpb, wo, g, b, wr, wrl, br):
    n = x.shape[0]
    row = lambda w_: pl.BlockSpec((TM, w_), lambda i: (i, 0))
    return pl.pallas_call(
        _merge_kernel,
        grid=(n // TM,),
        in_specs=[row(D_MODEL), row(ATTN_WIDTH), row(CONV_DIM),
                  _const_spec(wg.shape), _const_spec(wpa.shape), _const_spec(wpb.shape), _const_spec(wo.shape),
                  _const_spec(g.shape), _const_spec(b.shape),
                  _const_spec(wr.shape), _const_spec(wrl.shape), _const_spec(br.shape)],
        out_specs=[row(D_MODEL), row(LANES), row(LANES)],
        out_shape=[jax.ShapeDtypeStruct((n, D_MODEL), F32),
                   jax.ShapeDtypeStruct((n, LANES), I32), jax.ShapeDtypeStruct((n, LANES), F32)],
        compiler_params=_cparams(1),
        name="merge",
    )(x, a, yc, wg, wpa, wpb, wo, g, b, wr, wrl, br)


PAIR_W = 2 * LANES


def _moe_kernel(be_ref, na_ref, nv_ref, xs_ref, w1_ref, b1g_ref, b1l_ref, w2_ref, b2_ref, sel_ref, o_ref,
                w1g_s, w1l_s, w2_s):
    i = pl.program_id(0)
    active = i < na_ref[0]
    new_expert = jnp.logical_or(i == 0, be_ref[i] != be_ref[jnp.maximum(i - 1, 0)])

    @pl.when(jnp.logical_and(active, new_expert))
    def _():
        sel = sel_ref[...]
        for c in range(2 * D_EXPERT // PAIR_W):
            wc = w1_ref[:, c * PAIR_W:(c + 1) * PAIR_W].astype(BF16)
            d = jnp.dot(wc, sel, preferred_element_type=F32).astype(BF16)
            w1g_s[:, c * LANES:(c + 1) * LANES] = d[:, :LANES]
            w1l_s[:, c * LANES:(c + 1) * LANES] = d[:, LANES:]
        w2_s[...] = w2_ref[...].astype(BF16)

    @pl.when(active)
    def _():
        row = lax.broadcasted_iota(I32, (MOE_BM, D_MODEL), 0)
        xs = jnp.where(row < nv_ref[i], xs_ref[...], 0.0).astype(BF16)
        hg = jnp.dot(xs, w1g_s[...], preferred_element_type=F32) + b1g_ref[...]
        hl = jnp.dot(xs, w1l_s[...], preferred_element_type=F32) + b1l_ref[...]
        glu = jnp.minimum(hg, SWIGLU_LIMIT)
        lin = jnp.clip(hl, -SWIGLU_LIMIT, SWIGLU_LIMIT)
        act = glu * jax.nn.sigmoid(SWIGLU_ALPHA * glu) * (lin + 1.0)
        o_ref[...] = jnp.dot(act.astype(BF16), w2_s[...], preferred_element_type=F32) + b2_ref[...]

    @pl.when(jnp.logical_not(active))
    def _():
        o_ref[...] = jnp.zeros(o_ref.shape, F32)


def _pair_select():
    r = np.arange(PAIR_W)[:, None]
    c = np.arange(PAIR_W)[None, :]
    src = np.where(c < LANES, 2 * c, 2 * (c - LANES) + 1)
    return jnp.asarray(r == src, BF16)


def _moe(layer, blk_e, n_act, n_valid, xs, w1, b1g, b1l, w2, b2):
    wspec = lambda r, c: pl.BlockSpec((None, None, r, c), lambda i, be, na, nv: (layer, be[i], 0, 0))
    grid_spec = pltpu.PrefetchScalarGridSpec(
        num_scalar_prefetch=3,
        grid=(MOE_NBLK,),
        in_specs=[pl.BlockSpec((MOE_BM, D_MODEL), lambda i, be, na, nv: (i, 0)),
                  wspec(D_MODEL, 2 * D_EXPERT), wspec(1, D_EXPERT), wspec(1, D_EXPERT),
                  wspec(D_EXPERT, D_MODEL), wspec(1, D_MODEL),
                  pl.BlockSpec((PAIR_W, PAIR_W), lambda i, be, na, nv: (0, 0))],
        out_specs=pl.BlockSpec((MOE_BM, D_MODEL), lambda i, be, na, nv: (i, 0)),
        scratch_shapes=[pltpu.VMEM((D_MODEL, D_EXPERT), BF16), pltpu.VMEM((D_MODEL, D_EXPERT), BF16),
                        pltpu.VMEM((D_EXPERT, D_MODEL), BF16)],
    )
    return pl.pallas_call(
        _moe_kernel,
        grid_spec=grid_spec,
        out_shape=jax.ShapeDtypeStruct((MOE_ROWS, D_MODEL), F32),
        compiler_params=_cparams(1),
        name="moe",
    )(blk_e, n_act, n_valid, xs, w1, b1g, b1l, w2, b2, _pair_select())


SC_ROWS = 64


def _sc_gather(table, idx):
    n_idx = idx.shape[0]
    width = table.shape[1]
    info = plsc.get_sparse_core_info()
    n_cores = info.num_cores
    n_workers = n_cores * info.num_subcores
    per_worker = n_idx // n_workers
    assert per_worker * n_workers == n_idx and per_worker % SC_ROWS == 0
    mesh = plsc.VectorSubcoreMesh(core_axis_name="c", subcore_axis_name="s")

    @functools.partial(
        pl.kernel, mesh=mesh,
        out_type=jax.ShapeDtypeStruct((n_idx, width), table.dtype),
        scratch_types=[pltpu.VMEM((SC_ROWS,), I32), pltpu.VMEM((SC_ROWS, width), table.dtype),
                       pltpu.SemaphoreType.DMA],
    )
    def gather_kernel(table_hbm, idx_hbm, out_hbm, idx_v, rows_v, sem):
        base = (lax.axis_index("s") * n_cores + lax.axis_index("c")) * per_worker

        @pl.loop(0, per_worker // SC_ROWS)
        def _(j):
            off = base + j * SC_ROWS
            pltpu.sync_copy(idx_hbm.at[pl.ds(off, SC_ROWS)], idx_v)
            pltpu.async_copy(table_hbm.at[idx_v], rows_v, sem).wait()
            pltpu.sync_copy(rows_v, out_hbm.at[pl.ds(off, SC_ROWS)])

    return gather_kernel(table, idx)


SC_TOKENS = 32


def _sc_scatter_rows(x, dest_t, n_rows):
    n_tok, width = x.shape
    top_k = dest_t.shape[0]
    info = plsc.get_sparse_core_info()
    n_cores = info.num_cores
    n_workers = n_cores * info.num_subcores
    per_worker = n_tok // n_workers
    assert per_worker * n_workers == n_tok and per_worker % SC_TOKENS == 0
    mesh = plsc.VectorSubcoreMesh(core_axis_name="c", subcore_axis_name="s")

    @functools.partial(
        pl.kernel, mesh=mesh,
        out_type=jax.ShapeDtypeStruct((n_rows, width), x.dtype),
        scratch_types=[pltpu.VMEM((SC_TOKENS,), I32), pltpu.VMEM((SC_TOKENS, width), x.dtype)],
    )
    def scatter_kernel(x_hbm, idx_hbm, out_hbm, idx_v, rows_v):
        base = (lax.axis_index("s") * n_cores + lax.axis_index("c")) * per_worker

        @pl.loop(0, per_worker // SC_TOKENS)
        def _(j):
            t0 = base + j * SC_TOKENS
            pltpu.sync_copy(x_hbm.at[pl.ds(t0, SC_TOKENS)], rows_v)
            for k in range(top_k):
                pltpu.sync_copy(idx_hbm.at[pl.ds(k * n_tok + t0, SC_TOKENS)], idx_v)
                pltpu.sync_copy(rows_v, out_hbm.at[idx_v])

    return scatter_kernel(x, dest_t.reshape(-1))


def _route(top_e):
    flat_e = top_e.reshape(-1)
    onehot = (flat_e[:, None] == jnp.arange(N_EXPERTS, dtype=I32)[None, :]).astype(I32)
    csum = jnp.cumsum(onehot, axis=0)
    counts = csum[-1]
    rank = jnp.take_along_axis(csum, flat_e[:, None], axis=1)[:, 0] - 1
    padded = (counts + MOE_BM - 1) // MOE_BM * MOE_BM
    pad_end = jnp.cumsum(padded)
    pad_start = pad_end - padded
    dest_t = (pad_start[flat_e] + rank).reshape(N_TOK, TOP_K).T
    blk_start = jnp.arange(MOE_NBLK, dtype=I32) * MOE_BM
    blk_e = jnp.minimum(jnp.sum(blk_start[:, None] >= pad_end[None, :], axis=1), N_EXPERTS - 1).astype(I32)
    n_act = (pad_end[-1] // MOE_BM).astype(I32).reshape(1)
    n_valid = jnp.clip((pad_start + counts)[blk_e] - blk_start, 0, MOE_BM).astype(I32)
    return dest_t, blk_e, n_act, n_valid


def _ln2_kernel(x_ref, y_ref, tg_ref, g_ref, b_ref, o_ref):
    tg = tg_ref[...]
    y = tg[:, 0:1] * y_ref[0]
    for r in range(1, TOP_K):
        y = y + tg[:, r:r + 1] * y_ref[r]
    o_ref[...] = _layer_norm(DEEPNORM_ALPHA * x_ref[...] + y, g_ref[...], b_ref[...])


def _ln2(x1, yk, tg, g, b):
    n = x1.shape[0]
    return pl.pallas_call(
        _ln2_kernel,
        grid=(n // TM,),
        in_specs=[pl.BlockSpec((TM, D_MODEL), lambda i: (i, 0)),
                  pl.BlockSpec((TOP_K, TM, D_MODEL), lambda i: (0, i, 0)),
                  pl.BlockSpec((TM, LANES), lambda i: (i, 0)),
                  _const_spec(g.shape), _const_spec(b.shape)],
        out_specs=pl.BlockSpec((TM, D_MODEL), lambda i: (i, 0)),
        out_shape=jax.ShapeDtypeStruct((n, D_MODEL), F32),
        compiler_params=_cparams(1),
        name="ln2",
    )(x1, yk, tg, g, b)


def _rot_cols(w, n_heads):
    lead = w.shape[:-1]
    w4 = w.reshape(*lead, n_heads, 2, HEAD_DIM // 2)
    return jnp.concatenate([-w4[..., 1, :], w4[..., 0, :]], axis=-1).reshape(*lead, n_heads * HEAD_DIM)


def _pair_major(w):
    lead = w.shape[:-1]
    w4 = w.reshape(*lead, KV_HEADS, 2, HEAD_DIM)
    return jnp.swapaxes(w4, -3, -2).reshape(*lead, ATTN_WIDTH)


def _prep_weights(w_in, w_pa, w_router, b_router):
    c = np.cumsum((ATTN_WIDTH, KV_WIDTH, KV_WIDTH, IDX_WIDTH, IDX_DIM, IDX_HEADS,
                   CONV_DIM, CONV_DIM, CONV_DIM, D_MODEL, D_MODEL)).tolist()
    wq, wk, wv = w_in[..., :c[0]], w_in[..., c[0]:c[1]], w_in[..., c[1]:c[2]]
    wqi, wki, wwi = w_in[..., c[2]:c[3]], w_in[..., c[3]:c[4]], w_in[..., c[4]:c[5]]
    wconv = w_in[..., c[5]:c[8]]
    wgate = w_in[..., c[8]:]
    wki_r = _rot_cols(wki, 1)
    wwi = jnp.pad(wwi * (IDX_HEADS ** -0.5 * IDX_DIM ** -0.5), ((0, 0), (0, 0), (0, LANES - IDX_HEADS)))
    widx = jnp.concatenate([wqi, _rot_cols(wqi, IDX_HEADS), jnp.tile(wki, (1, 1, IDX_HEADS)),
                            jnp.tile(wki_r, (1, 1, IDX_HEADS)), wwi], axis=-1)
    widx_hi, w1lo = _split_hi_lo(widx)
    w1cat = jnp.concatenate(
        [_pair_major(wq).astype(BF16), _pair_major(_rot_cols(wq, N_HEADS)).astype(BF16), wk.astype(BF16),
         _rot_cols(wk, KV_HEADS).astype(BF16), wv.astype(BF16), widx_hi, wconv.astype(BF16)], axis=-1)
    wpa = jnp.swapaxes(w_pa.reshape(DEPTH, KV_HEADS, 2, HEAD_DIM, D_MODEL), 1, 2).reshape(DEPTH, ATTN_WIDTH, D_MODEL)
    wr, wrl = _split_hi_lo(jnp.pad(w_router, ((0, 0), (0, 0), (0, LANES - N_EXPERTS))))
    br = jnp.pad(b_router, ((0, 0), (0, LANES - N_EXPERTS)), constant_values=-jnp.inf)[:, None, :]
    return w1cat, w1lo, wgate.astype(BF16), wpa.astype(BF16), wr, wrl, br


def _rope_tables():
    pos = jnp.concatenate([jnp.tile(jnp.arange(SEQ, dtype=I32), BATCH),
                           jnp.tile(PAST_LEN + jnp.arange(DEC_SEQ, dtype=I32), DEC_BATCH)])
    half = HEAD_DIM // 2
    inv = ROPE_THETA ** (-jnp.arange(half, dtype=F32) / half)
    ang = pos.astype(F32)[:, None] * inv[None, :]
    return jnp.tile(jnp.cos(ang), (1, LANES // half)), jnp.tile(jnp.sin(ang), (1, LANES // half))


def kernel(x_prompt, x_sample, cache_k, cache_v, cache_kidx, state_conv, page_table, w_in, w_conv, w_pa, w_pb,
           w_o, ln1_g, ln1_b, w_router, b_router, w1, b1, w2, b2, ln2_g, ln2_b):
    w1cat, w1lo, wgate, wpa, wr, wrl, br = _prep_weights(w_in, w_pa, w_router, b_router)
    wpb = w_pb.astype(BF16)
    wo = w_o.astype(BF16)
    wc = jnp.pad(w_conv, ((0, 0), (0, 8 - w_conv.shape[1]), (0, 0)))
    b1g = b1[..., 0::2][:, :, None, :]
    b1l = b1[..., 1::2][:, :, None, :]
    b2r = b2[:, :, None, :]
    cos, sin = _rope_tables()
    zrows = jnp.zeros((DEPTH, DEC_BATCH, DEC_SEQ - 2, CONV_DIM), F32)
    h1 = jnp.concatenate([state_conv[:, :, 1:2], zrows[:, :, :1], zrows], axis=2).reshape(DEPTH, N_S, CONV_DIM)
    h2 = jnp.concatenate([state_conv, zrows], axis=2).reshape(DEPTH, N_S, CONV_DIM)
    ckT = jnp.transpose(cache_k, (0, 1, 3, 4, 2)).reshape(DEPTH, -1, KV_WIDTH, PAGE_SIZE)
    cvT = jnp.transpose(cache_v, (0, 1, 3, 4, 2)).reshape(DEPTH, -1, KV_WIDTH, PAGE_SIZE)
    ckiT = jnp.transpose(cache_kidx, (0, 1, 3, 2))
    pt = page_table.reshape(-1).astype(I32)

    x = jnp.concatenate([x_prompt.reshape(N_P, D_MODEL), x_sample.reshape(N_S, D_MODEL)], axis=0)
    ks, vs, kis, convs = [], [], [], []
    for l in range(DEPTH):
        q, qi, wi, k, v, ki4, kb, vb, kix, yc, u = _inproj(x, w1cat[l], w1lo[l], cos, sin, wc[l], h1[l], h2[l])
        a = _attn_sample(l, pt, q, qi, wi, k, v, ki4, ckT, cvT, ckiT)
        a = _attn_prompt(a, q, qi, wi, kb, vb, kix)

        x1, te, tg = _merge(x, a, yc, wgate[l], wpa[l], wpb[l], wo[l],
                            ln1_g[l][None], ln1_b[l][None], wr[l], wrl[l], br[l])
        dest_t, blk_e, n_act, n_valid = _route(te[:, :TOP_K])
        xs = _sc_scatter_rows(x1, dest_t, MOE_ROWS)
        y_rows = _moe(l, blk_e, n_act, n_valid, xs, w1, b1g, b1l, w2, b2r)
        yk = _sc_gather(y_rows, dest_t.reshape(-1)).reshape(TOP_K, N_TOK, D_MODEL)
        x = _ln2(x1, yk, tg, ln2_g[l][None], ln2_b[l][None])

        ks.append(k)
        vs.append(v)
        kis.append(ki4[:, :IDX_DIM])
        convs.append((u[:N_P].reshape(BATCH, SEQ, CONV_DIM)[:, -2:],
                      u[N_P:].reshape(DEC_BATCH, DEC_SEQ, CONV_DIM)[:, -2:]))

    def split(parts, tail):
        prompt = jnp.stack([a_[:N_P] for a_ in parts], axis=0).reshape(DEPTH, BATCH, SEQ, *tail)
        sample = jnp.stack([a_[N_P:] for a_ in parts], axis=0).reshape(DEPTH, DEC_BATCH, DEC_SEQ, *tail)
        return prompt, sample

    k_p, k_s = split(ks, (KV_HEADS, HEAD_DIM))
    v_p, v_s = split(vs, (KV_HEADS, HEAD_DIM))
    ki_p, ki_s = split(kis, (IDX_DIM,))
    conv_p = jnp.stack([c_[0] for c_ in convs], axis=0)
    conv_s = jnp.stack([c_[1] for c_ in convs], axis=0)
    y_p = x[:N_P].reshape(BATCH, SEQ, D_MODEL)
    y_s = x[N_P:].reshape(DEC_BATCH, DEC_SEQ, D_MODEL)
    return (y_p, y_s, k_p, v_p, ki_p, conv_p, k_s, v_s, ki_s, conv_s)
```

```python
import functools

import numpy as np
import jax
import jax.numpy as jnp
from jax import lax
from jax.experimental import pallas as pl
from jax.experimental.pallas import tpu as pltpu
from jax.experimental.pallas import tpu_sc as plsc

F32 = jnp.float32
BF16 = jnp.bfloat16
I32 = jnp.int32

D_MODEL = 1024
BATCH = 4
SEQ = 4096
DEPTH = 4
DEC_BATCH = 128
DEC_SEQ = 8
PAGE_SIZE = 128
N_PAGES = 16
PAST_LEN = N_PAGES * PAGE_SIZE
N_HEADS = 8
HEAD_DIM = 64
KV_HEADS = 4
IDX_HEADS = 4
IDX_DIM = 64
TOPK = 256
CONV_DIM = 512
N_EXPERTS = 32
TOP_K = 4
D_EXPERT = 1024
SWIGLU_LIMIT = 7.0
SWIGLU_ALPHA = 1.702
ROPE_THETA = 10000.0
LN_EPS = 1e-5
ATTN_WIDTH = N_HEADS * HEAD_DIM
KV_WIDTH = KV_HEADS * HEAD_DIM
IDX_WIDTH = IDX_HEADS * IDX_DIM
DEEPNORM_ALPHA = (2 * DEPTH) ** 0.25

N_P = BATCH * SEQ
N_S = DEC_BATCH * DEC_SEQ
N_TOK = N_P + N_S

LANES = 128
VMEM_LIMIT = 56 * 1024 * 1024
INT_MIN = -2 ** 31

_SEC = {}
_off = 0
for _name, _w in (("q", 512), ("qr", 512), ("k", 256), ("kr", 256), ("v", 256), ("qi", 256), ("qir", 256),
                  ("kik", 128), ("wi", 128), ("cb", 512), ("cc", 512), ("cx", 512)):
    _SEC[_name] = (_off, _off + _w)
    _off += _w
W1_COLS = _off
_SEC_LO = {}
_off = 0
for _name in ("qi", "qir", "kik", "wi"):
    _w = _SEC[_name][1] - _SEC[_name][0]
    _SEC_LO[_name] = (_off, _off + _w)
    _off += _w
W1_LO_COLS = _off

TM = 512
TQ = 256
MOE_BM = 256
MOE_ROWS = N_TOK * TOP_K + N_EXPERTS * MOE_BM
MOE_NBLK = MOE_ROWS // MOE_BM


def _cparams(n_axes):
    return pltpu.CompilerParams(dimension_semantics=("arbitrary",) * n_axes, vmem_limit_bytes=VMEM_LIMIT)


def _const_spec(shape):
    nd = len(shape)
    return pl.BlockSpec(shape, lambda *_: (0,) * nd)


def _split_hi_lo(x):
    hi = lax.bitcast_convert_type(lax.bitcast_convert_type(x, I32) & jnp.int32(-65536), F32)
    return hi.astype(BF16), (x - hi).astype(BF16)


def _inproj_kernel(x_ref, w_ref, wlo_ref, cos_ref, sin_ref, wc_ref, h1_ref, h2_ref,
                   q_ref, qi_ref, wi_ref, k_ref, v_ref, ki_ref, kb_ref, vb_ref, kix_ref, yc_ref, u_ref, tail_ref):
    x = x_ref[...]
    xb = x.astype(BF16)
    xhi, xlo = _split_hi_lo(x)

    def mm(name):
        a, b = _SEC[name]
        if name not in _SEC_LO:
            return jnp.dot(xb, w_ref[:, a:b], preferred_element_type=F32)
        la, lb = _SEC_LO[name]
        return (jnp.dot(xhi, w_ref[:, a:b], preferred_element_type=F32)
                + jnp.dot(xlo, w_ref[:, a:b], preferred_element_type=F32)
                + jnp.dot(xhi, wlo_ref[:, la:lb], preferred_element_type=F32)
                + jnp.dot(xlo, wlo_ref[:, la:lb], preferred_element_type=F32))

    cos = cos_ref[...]
    sin = sin_ref[...]
    cos2 = jnp.concatenate([cos, cos], axis=1)
    sin2 = jnp.concatenate([sin, sin], axis=1)
    cos4 = jnp.concatenate([cos2, cos2], axis=1)
    sin4 = jnp.concatenate([sin2, sin2], axis=1)
    q_ref[...] = mm("q") * cos4 + mm("qr") * sin4
    qi_ref[...] = mm("qi") * cos2 + mm("qir") * sin2
    k = mm("k") * cos2 + mm("kr") * sin2
    k_ref[...] = k
    kb_ref[...] = k.astype(BF16)
    hk = mm("kik")
    hk_sw = pltpu.roll(hk, IDX_DIM, axis=1)
    first = lax.broadcasted_iota(I32, hk.shape, 1) < IDX_DIM
    ki2 = jnp.where(first, hk * cos + hk_sw * sin, hk_sw * cos + hk * sin)
    ki = jnp.concatenate([ki2, ki2], axis=1)
    ki_ref[...] = ki
    ki_hi, ki_lo = _split_hi_lo(ki)
    kix_ref[...] = jnp.concatenate([ki_hi[:, :LANES], ki_lo[:, :LANES]], axis=1)
    v = mm("v")
    v_ref[...] = v
    vb_ref[...] = v.astype(BF16)
    wi_ref[...] = mm("wi")

    i = pl.program_id(0)
    u = mm("cc") * mm("cx")
    u_ref[...] = u

    @pl.when(i == 0)
    def _():
        tail_ref[...] = jnp.zeros(tail_ref.shape, F32)

    row = lax.broadcasted_iota(I32, (TM, CONV_DIM), 0)
    back1 = pltpu.roll(u, 1, axis=0)
    back2 = pltpu.roll(u, 2, axis=0)
    mid_seq = lax.rem(i * TM, SEQ) != 0
    tail = jnp.where(mid_seq, tail_ref[...], 0.0)
    p1 = jnp.where(row == 0, tail[7:8], back1)
    p2 = jnp.where(row == 0, tail[6:7], jnp.where(row == 1, tail[7:8], back2))
    pos = lax.rem(row, DEC_SEQ)
    s1 = jnp.where(pos == 0, h1_ref[...], back1)
    s2 = jnp.where(pos < 2, h2_ref[...], back2)
    is_sample = i >= N_P // TM
    um1 = jnp.where(is_sample, s1, p1)
    um2 = jnp.where(is_sample, s2, p2)
    tail_ref[...] = u[TM - 8:, :]
    wc = wc_ref[...]
    yc_ref[...] = mm("cb") * (wc[0:1] * um2 + wc[1:2] * um1 + wc[2:3] * u)


def _inproj(x, w, wlo, cos, sin, wc, h1, h2):
    n = x.shape[0]
    row = lambda w_: pl.BlockSpec((TM, w_), lambda i: (i, 0))
    srow = pl.BlockSpec((TM, CONV_DIM), lambda i: (jnp.maximum(i - N_P // TM, 0), 0))
    widths = (512, 256, 128, 256, 256, 256, 256, 256, 256, 512, 512)
    dtypes = (F32, F32, F32, F32, F32, F32, BF16, BF16, BF16, F32, F32)
    return pl.pallas_call(
        _inproj_kernel,
        grid=(n // TM,),
        in_specs=[row(D_MODEL), _const_spec((D_MODEL, W1_COLS)), _const_spec((D_MODEL, W1_LO_COLS)),
                  row(LANES), row(LANES), _const_spec(wc.shape), srow, srow],
        out_specs=[row(w_) for w_ in widths],
        out_shape=[jax.ShapeDtypeStruct((n, w_), d_) for w_, d_ in zip(widths, dtypes)],
        scratch_shapes=[pltpu.VMEM((8, CONV_DIM), F32)],
        compiler_params=_cparams(1),
        name="inproj",
    )(x, w, wlo, cos, sin, wc, h1, h2)


def _sort_key(score, valid):
    bits = pltpu.bitcast(score, I32)
    key = jnp.where(bits < 0, bits ^ jnp.int32(0x7FFFFFFF), bits)
    key = jnp.where(score == 0.0, jnp.int32(0), key)
    return jnp.where(valid, key, jnp.int32(INT_MIN))


def _count(mask):
    return jnp.sum(jnp.where(mask, 1.0, 0.0), axis=1, keepdims=True)


def _select_topk(key_ref, c0_ref, s_idx, n_keys, two_bits=False):
    rows = key_ref.shape[0]
    kf = float(TOPK)

    def thr_step(it, t):
        cand = t + jnp.left_shift(jnp.int32(1), 31 - it)
        cnt = _count(key_ref[...] >= cand)
        return jnp.where(cnt >= kf, cand, t)

    def thr_step2(it, t):
        d = jnp.left_shift(jnp.int32(1), 30 - 2 * it)
        c1 = t + d
        c2 = c1 + d
        c3 = c2 + d
        key = key_ref[...]
        n1, n2, n3 = _count(key >= c1), _count(key >= c2), _count(key >= c3)
        return jnp.where(n3 >= kf, c3, jnp.where(n2 >= kf, c2, jnp.where(n1 >= kf, c1, t)))

    thr0 = jnp.full((rows, 1), INT_MIN, I32)
    thr = lax.fori_loop(0, 16, thr_step2, thr0) if two_bits else lax.fori_loop(0, 32, thr_step, thr0)
    key = key_ref[...]
    live = key > jnp.int32(INT_MIN)
    n_gt = _count(key > thr)
    n_ge = _count(jnp.logical_and(key >= thr, live))
    c0_ref[...] = jnp.full(c0_ref.shape, n_keys, I32)

    @pl.when(jnp.max(n_ge) > kf)
    def _():
        need = kf - n_gt
        nbits = int(n_keys).bit_length()

        def idx_step(it, c):
            cand = c + jnp.left_shift(jnp.int32(1), nbits - 1 - it)
            eq = jnp.logical_and(key_ref[...] == thr, s_idx < cand)
            return jnp.where(_count(eq) < need, cand, c)

        c_last = lax.fori_loop(0, nbits, idx_step, jnp.zeros((rows, 1), I32))
        c0_ref[...] = jnp.broadcast_to(c_last, c0_ref.shape)

    c0 = c0_ref[:, 0:1]
    tie = jnp.logical_and(key == thr, s_idx <= c0)
    return jnp.logical_and(jnp.logical_or(key > thr, tie), live)


def _head_mask(x, g):
    lane = lax.broadcasted_iota(I32, x.shape, 1)
    return jnp.where(jnp.right_shift(lane, 6) == g, x, 0.0)


_NT = (((1,), (1,)), ((), ()))


def _attn_p_kernel(q_ref, qi_ref, wi_ref, k_ref, v_ref, kix_ref, abuf_ref, o_ref, key_ref, neg_ref, c0_ref,
                   *, n_keys, q_base):
    del abuf_ref
    t0 = q_base + pl.program_id(1) * TQ
    qi = qi_ref[...]
    wi = wi_ref[...]
    kix = kix_ref[...]
    lane_blk = jnp.right_shift(lax.broadcasted_iota(I32, (TQ, IDX_WIDTH), 1), 6)
    score = jnp.zeros((TQ, n_keys), F32)
    for h in range(IDX_HEADS):
        x = _head_mask(qi, h)
        x = x + pltpu.roll(x, 2 * IDX_DIM, axis=1)
        x = x + pltpu.roll(x, IDX_DIM, axis=1)
        x_hi, x_lo = _split_hi_lo(x)
        qx = jnp.where((lane_blk & 1) == 0, x_hi, x_lo)
        rel = lax.dot_general(qx, kix, _NT, preferred_element_type=F32)
        score = score + wi[:, h:h + 1] * jnp.maximum(rel, 0.0)
    s_idx = lax.broadcasted_iota(I32, (TQ, n_keys), 1)
    t_idx = t0 + lax.broadcasted_iota(I32, (TQ, n_keys), 0)
    key_ref[...] = _sort_key(score, s_idx <= t_idx)
    sel = _select_topk(key_ref, c0_ref, s_idx, n_keys)
    neg_ref[...] = jnp.where(sel, 0.0, -jnp.inf)

    kb = k_ref[...]
    vb = v_ref[...]
    for hq in range(2):
        qh = q_ref[:, hq * KV_WIDTH:(hq + 1) * KV_WIDTH]
        acc = jnp.zeros((TQ, KV_WIDTH), F32)
        for g in range(KV_HEADS):
            qg = (_head_mask(qh, g) * HEAD_DIM ** -0.5).astype(BF16)
            lg = lax.dot_general(qg, kb, _NT, preferred_element_type=F32) + neg_ref[...]
            p = jnp.exp(lg - jnp.max(lg, axis=1, keepdims=True))
            inv = 1.0 / jnp.sum(p, axis=1, keepdims=True)
            pv = jnp.dot(p.astype(BF16), vb, preferred_element_type=F32)
            acc = acc + _head_mask(pv * inv, g)
        o_ref[:, hq * KV_WIDTH:(hq + 1) * KV_WIDTH] = acc


def _attn_p_group(a_buf, q, qi, wi, kb, vb, kix, q_lo, q_hi, n_keys):
    nq = (q_hi - q_lo) // TQ
    qrow = lambda w_: pl.BlockSpec((TQ, w_), lambda b, i: ((b * SEQ + q_lo) // TQ + i, 0))
    krow = pl.BlockSpec((None, n_keys, KV_WIDTH), lambda b, i: (b, 0, 0))
    return pl.pallas_call(
        functools.partial(_attn_p_kernel, n_keys=n_keys, q_base=q_lo),
        grid=(BATCH, nq),
        in_specs=[qrow(ATTN_WIDTH), qrow(IDX_WIDTH), qrow(LANES), krow, krow, krow,
                  pl.BlockSpec(memory_space=pl.ANY)],
        out_specs=qrow(ATTN_WIDTH),
        out_shape=jax.ShapeDtypeStruct(a_buf.shape, F32),
        input_output_aliases={6: 0},
        scratch_shapes=[pltpu.VMEM((TQ, n_keys), I32), pltpu.VMEM((TQ, n_keys), F32), pltpu.VMEM((TQ, LANES), I32)],
        compiler_params=_cparams(2),
        name=f"attn_p{n_keys}",
    )(q, qi, wi, kb, vb, kix, a_buf)


P_GROUP = 256
_P_GROUPS = tuple((lo, lo + P_GROUP, lo + P_GROUP) for lo in range(0, SEQ, P_GROUP))


def _attn_prompt(a_buf, q, qi, wi, kb, vb, kix):
    seq3 = lambda a_: a_[:N_P].reshape(BATCH, SEQ, KV_WIDTH)
    kb, vb, kix = seq3(kb), seq3(vb), seq3(kix)
    for lo, hi, nk in _P_GROUPS:
        a_buf = _attn_p_group(a_buf, q, qi, wi, kb, vb, kix, lo, hi, nk)
    return a_buf


S_KEYS = PAST_LEN + LANES


S_PAIR = 4
S_ROWS = S_PAIR * DEC_SEQ


def _attn_s_kernel(pt_ref, q_ref, qi_ref, wi_ref, kn_ref, vn_ref, kin_ref, *rest):
    n_pg = S_PAIR * N_PAGES
    kT, vT, kiT = rest[0:n_pg], rest[n_pg:2 * n_pg], rest[2 * n_pg:3 * n_pg]
    o_ref, kTs, vTs, kiTs, key_ref, c0_ref = rest[3 * n_pg:]
    del pt_ref
    pad = jnp.zeros((LANES - DEC_SEQ, KV_WIDTH), F32)
    s_idx = lax.broadcasted_iota(I32, (DEC_SEQ, S_KEYS), 1)
    t_idx = PAST_LEN + lax.broadcasted_iota(I32, (DEC_SEQ, S_KEYS), 0)
    new_kv = []
    for s in range(S_PAIR):
        rows = slice(s * DEC_SEQ, (s + 1) * DEC_SEQ)
        for j in range(N_PAGES):
            sl = slice(j * PAGE_SIZE, (j + 1) * PAGE_SIZE)
            kTs[s, :, sl] = kT[s * N_PAGES + j][...].astype(BF16)
            vTs[s, :, sl] = vT[s * N_PAGES + j][...].astype(BF16)
            kiTs[s, :, sl] = kiT[s * N_PAGES + j][...].astype(BF16)
        kn = jnp.concatenate([kn_ref[rows, :], pad], axis=0).astype(BF16)
        vn = jnp.concatenate([vn_ref[rows, :], pad], axis=0).astype(BF16)
        kin = jnp.concatenate([kin_ref[rows, :], pad], axis=0)[:, :IDX_DIM].astype(BF16)
        new_kv.append((kn, vn))

        qi = qi_ref[rows, :]
        wi = wi_ref[rows, :]
        qis = jnp.concatenate([qi[:, h * IDX_DIM:(h + 1) * IDX_DIM] for h in range(IDX_HEADS)], axis=0).astype(BF16)
        rel = jnp.concatenate([jnp.dot(qis, kiTs[s], preferred_element_type=F32),
                               lax.dot_general(qis, kin, _NT, preferred_element_type=F32)], axis=1)
        rel = jnp.maximum(rel, 0.0)
        score = jnp.zeros((DEC_SEQ, S_KEYS), F32)
        for h in range(IDX_HEADS):
            score = score + wi[:, h:h + 1] * rel[h * DEC_SEQ:(h + 1) * DEC_SEQ]
        key_ref[rows, :] = _sort_key(score, s_idx <= t_idx)

    s_idx_all = lax.broadcasted_iota(I32, (S_ROWS, S_KEYS), 1)
    sel = _select_topk(key_ref, c0_ref, s_idx_all, S_KEYS, two_bits=True)
    neg_all = jnp.where(sel, 0.0, -jnp.inf)

    for s in range(S_PAIR):
        rows = slice(s * DEC_SEQ, (s + 1) * DEC_SEQ)
        kn, vn = new_kv[s]
        neg = neg_all[rows]
        q = q_ref[rows, :]
        qs = jnp.concatenate([_head_mask(q[:, hq * KV_WIDTH:(hq + 1) * KV_WIDTH], g)
                              for hq in range(2) for g in range(KV_HEADS)], axis=0).astype(BF16)
        lg = jnp.concatenate([jnp.dot(qs, kTs[s], preferred_element_type=F32),
                              lax.dot_general(qs, kn, _NT, preferred_element_type=F32)], axis=1)
        lg = lg * HEAD_DIM ** -0.5 + jnp.concatenate([neg] * N_HEADS, axis=0)
        p = jnp.exp(lg - jnp.max(lg, axis=1, keepdims=True))
        inv = 1.0 / jnp.sum(p, axis=1, keepdims=True)
        pb = p.astype(BF16)
        pv = (lax.dot_general(pb[:, :PAST_LEN], vTs[s], _NT, preferred_element_type=F32)
              + jnp.dot(pb[:, PAST_LEN:], vn, preferred_element_type=F32)) * inv
        for hq in range(2):
            acc = jnp.zeros((DEC_SEQ, KV_WIDTH), F32)
            for g in range(KV_HEADS):
                r = (hq * KV_HEADS + g) * DEC_SEQ
                acc = acc + _head_mask(pv[r:r + DEC_SEQ], g)
            o_ref[rows, hq * KV_WIDTH:(hq + 1) * KV_WIDTH] = acc


def _attn_sample(layer, page_table, q, qi, wi, k, v, ki, ckT, cvT, ckiT):
    srow = lambda w_: pl.BlockSpec((S_ROWS, w_), lambda b, pt: (N_P // S_ROWS + b, 0))

    def page(rows, s, j):
        return pl.BlockSpec((None, None, rows, PAGE_SIZE),
                            lambda b, pt: (layer, pt[(b * S_PAIR + s) * N_PAGES + j], 0, 0))

    pages = lambda rows: [page(rows, s, j) for s in range(S_PAIR) for j in range(N_PAGES)]
    in_specs = [srow(ATTN_WIDTH), srow(IDX_WIDTH), srow(LANES), srow(KV_WIDTH), srow(KV_WIDTH), srow(KV_WIDTH)]
    in_specs += pages(KV_WIDTH) + pages(KV_WIDTH) + pages(IDX_DIM)
    n_pg = S_PAIR * N_PAGES
    grid_spec = pltpu.PrefetchScalarGridSpec(
        num_scalar_prefetch=1,
        grid=(DEC_BATCH // S_PAIR,),
        in_specs=in_specs,
        out_specs=srow(ATTN_WIDTH),
        scratch_shapes=[pltpu.VMEM((S_PAIR, KV_WIDTH, PAST_LEN), BF16), pltpu.VMEM((S_PAIR, KV_WIDTH, PAST_LEN), BF16),
                        pltpu.VMEM((S_PAIR, IDX_DIM, PAST_LEN), BF16), pltpu.VMEM((S_ROWS, S_KEYS), I32),
                        pltpu.VMEM((S_ROWS, LANES), I32)],
    )
    return pl.pallas_call(
        _attn_s_kernel,
        grid_spec=grid_spec,
        out_shape=jax.ShapeDtypeStruct((N_TOK, ATTN_WIDTH), F32),
        compiler_params=_cparams(1),
        name="attn_s",
    )(page_table, q, qi, wi, k, v, ki, *([ckT] * n_pg), *([cvT] * n_pg), *([ckiT] * n_pg))


def _layer_norm(y, g, b):
    mu = jnp.mean(y, axis=1, keepdims=True)
    d = y - mu
    var = jnp.mean(d * d, axis=1, keepdims=True)
    return d * lax.rsqrt(var + LN_EPS) * g + b


def _merge_kernel(x_ref, a_ref, yc_ref, wg_ref, wpa_ref, wpb_ref, wo_ref,
                  g_ref, b_ref, wr_ref, wrl_ref, br_ref, x1_ref, te_ref, tg_ref):
    x = x_ref[...]
    gates = jax.nn.sigmoid(jnp.dot(x.astype(BF16), wg_ref[...], preferred_element_type=F32))
    pa = jnp.dot(a_ref[...].astype(BF16), wpa_ref[...], preferred_element_type=F32)
    pb = jnp.dot(yc_ref[...].astype(BF16), wpb_ref[...], preferred_element_type=F32)
    merged = gates[:, :D_MODEL] * pa + gates[:, D_MODEL:] * pb
    mix = jnp.dot(merged.astype(BF16), wo_ref[...], preferred_element_type=F32)
    x1 = _layer_norm(DEEPNORM_ALPHA * x + mix, g_ref[...], b_ref[...])
    x1_ref[...] = x1

    x1h, x1l = _split_hi_lo(x1)
    wrh = wr_ref[...]
    wrl = wrl_ref[...]
    lg = (jnp.dot(x1h, wrh, preferred_element_type=F32) + jnp.dot(x1l, wrh, preferred_element_type=F32)
          + jnp.dot(x1h, wrl, preferred_element_type=F32) + jnp.dot(x1l, wrl, preferred_element_type=F32)
          + br_ref[...])
    lane = lax.broadcasted_iota(I32, lg.shape, 1)
    te = jnp.zeros(lg.shape, I32)
    vals = []
    for r in range(TOP_K):
        m = jnp.max(lg, axis=1, keepdims=True)
        idx = jnp.min(jnp.where(lg == m, lane, LANES), axis=1, keepdims=True)
        te = jnp.where(lane == r, idx, te)
        vals.append(m)
        lg = jnp.where(lane == idx, -jnp.inf, lg)
    ex = [jnp.exp(v_ - vals[0]) for v_ in vals]
    inv = 1.0 / (ex[0] + ex[1] + ex[2] + ex[3])
    tg = jnp.zeros(lg.shape, F32)
    for r in range(TOP_K):
        tg = jnp.where(lane == r, ex[r] * inv, tg)
    te_ref[...] = te
    tg_ref[...] = tg


def _merge(x, a, yc, wg, wpa, wpb, wo, g, b, wr, wrl, br):
    n = x.shape[0]
    row = lambda w_: pl.BlockSpec((TM, w_), lambda i: (i, 0))
    return pl.pallas_call(
        _merge_kernel,
        grid=(n // TM,),
        in_specs=[row(D_MODEL), row(ATTN_WIDTH), row(CONV_DIM),
                  _const_spec(wg.shape), _const_spec(wpa.shape), _const_spec(wpb.shape), _const_spec(wo.shape),
                  _const_spec(g.shape), _const_spec(b.shape),
                  _const_spec(wr.shape), _const_spec(wrl.shape), _const_spec(br.shape)],
        out_specs=[row(D_MODEL), row(LANES), row(LANES)],
        out_shape=[jax.ShapeDtypeStruct((n, D_MODEL), F32),
                   jax.ShapeDtypeStruct((n, LANES), I32), jax.ShapeDtypeStruct((n, LANES), F32)],
        compiler_params=_cparams(1),
        name="merge",
    )(x, a, yc, wg, wpa, wpb, wo, g, b, wr, wrl, br)


PAIR_W = 2 * LANES


def _moe_kernel(be_ref, na_ref, nv_ref, xs_ref, w1_ref, b1g_ref, b1l_ref, w2_ref, b2_ref, sel_ref, o_ref,
                w1g_s, w1l_s, w2_s):
    i = pl.program_id(0)
    active = i < na_ref[0]
    new_expert = jnp.logical_or(i == 0, be_ref[i] != be_ref[jnp.maximum(i - 1, 0)])

    @pl.when(jnp.logical_and(active, new_expert))
    def _():
        sel = sel_ref[...]
        for c in range(2 * D_EXPERT // PAIR_W):
            wc = w1_ref[:, c * PAIR_W:(c + 1) * PAIR_W].astype(BF16)
            d = jnp.dot(wc, sel, preferred_element_type=F32).astype(BF16)
            w1g_s[:, c * LANES:(c + 1) * LANES] = d[:, :LANES]
            w1l_s[:, c * LANES:(c + 1) * LANES] = d[:, LANES:]
        w2_s[...] = w2_ref[...].astype(BF16)

    @pl.when(active)
    def _():
        row = lax.broadcasted_iota(I32, (MOE_BM, D_MODEL), 0)
        xs = jnp.where(row < nv_ref[i], xs_ref[...], 0.0).astype(BF16)
        hg = jnp.dot(xs, w1g_s[...], preferred_element_type=F32) + b1g_ref[...]
        hl = jnp.dot(xs, w1l_s[...], preferred_element_type=F32) + b1l_ref[...]
        glu = jnp.minimum(hg, SWIGLU_LIMIT)
        lin = jnp.clip(hl, -SWIGLU_LIMIT, SWIGLU_LIMIT)
        act = glu * jax.nn.sigmoid(SWIGLU_ALPHA * glu) * (lin + 1.0)
        o_ref[...] = jnp.dot(act.astype(BF16), w2_s[...], preferred_element_type=F32) + b2_ref[...]

    @pl.when(jnp.logical_not(active))
    def _():
        o_ref[...] = jnp.zeros(o_ref.shape, F32)


def _pair_select():
    r = np.arange(PAIR_W)[:, None]
    c = np.arange(PAIR_W)[None, :]
    src = np.where(c < LANES, 2 * c, 2 * (c - LANES) + 1)
    return jnp.asarray(r == src, BF16)


def _moe(layer, blk_e, n_act, n_valid, xs, w1, b1g, b1l, w2, b2):
    wspec = lambda r, c: pl.BlockSpec((None, None, r, c), lambda i, be, na, nv: (layer, be[i], 0, 0))
    grid_spec = pltpu.PrefetchScalarGridSpec(
        num_scalar_prefetch=3,
        grid=(MOE_NBLK,),
        in_specs=[pl.BlockSpec((MOE_BM, D_MODEL), lambda i, be, na, nv: (i, 0)),
                  wspec(D_MODEL, 2 * D_EXPERT), wspec(1, D_EXPERT), wspec(1, D_EXPERT),
                  wspec(D_EXPERT, D_MODEL), wspec(1, D_MODEL),
                  pl.BlockSpec((PAIR_W, PAIR_W), lambda i, be, na, nv: (0, 0))],
        out_specs=pl.BlockSpec((MOE_BM, D_MODEL), lambda i, be, na, nv: (i, 0)),
        scratch_shapes=[pltpu.VMEM((D_MODEL, D_EXPERT), BF16), pltpu.VMEM((D_MODEL, D_EXPERT), BF16),
                        pltpu.VMEM((D_EXPERT, D_MODEL), BF16)],
    )
    return pl.pallas_call(
        _moe_kernel,
        grid_spec=grid_spec,
        out_shape=jax.ShapeDtypeStruct((MOE_ROWS, D_MODEL), F32),
        compiler_params=_cparams(1),
        name="moe",
    )(blk_e, n_act, n_valid, xs, w1, b1g, b1l, w2, b2, _pair_select())


SC_ROWS = 64


def _sc_gather(table, idx):
    n_idx = idx.shape[0]
    width = table.shape[1]
    info = plsc.get_sparse_core_info()
    n_cores = info.num_cores
    n_workers = n_cores * info.num_subcores
    per_worker = n_idx // n_workers
    assert per_worker * n_workers == n_idx and per_worker % SC_ROWS == 0
    mesh = plsc.VectorSubcoreMesh(core_axis_name="c", subcore_axis_name="s")

    @functools.partial(
        pl.kernel, mesh=mesh,
        out_type=jax.ShapeDtypeStruct((n_idx, width), table.dtype),
        scratch_types=[pltpu.VMEM((SC_ROWS,), I32), pltpu.VMEM((SC_ROWS, width), table.dtype),
                       pltpu.SemaphoreType.DMA],
    )
    def gather_kernel(table_hbm, idx_hbm, out_hbm, idx_v, rows_v, sem):
        base = (lax.axis_index("s") * n_cores + lax.axis_index("c")) * per_worker

        @pl.loop(0, per_worker // SC_ROWS)
        def _(j):
            off = base + j * SC_ROWS
            pltpu.sync_copy(idx_hbm.at[pl.ds(off, SC_ROWS)], idx_v)
            pltpu.async_copy(table_hbm.at[idx_v], rows_v, sem).wait()
            pltpu.sync_copy(rows_v, out_hbm.at[pl.ds(off, SC_ROWS)])

    return gather_kernel(table, idx)


SC_TOKENS = 32


def _sc_scatter_rows(x, dest_t, n_rows):
    n_tok, width = x.shape
    top_k = dest_t.shape[0]
    info = plsc.get_sparse_core_info()
    n_cores = info.num_cores
    n_workers = n_cores * info.num_subcores
    per_worker = n_tok // n_workers
    assert per_worker * n_workers == n_tok and per_worker % SC_TOKENS == 0
    mesh = plsc.VectorSubcoreMesh(core_axis_name="c", subcore_axis_name="s")

    @functools.partial(
        pl.kernel, mesh=mesh,
        out_type=jax.ShapeDtypeStruct((n_rows, width), x.dtype),
        scratch_types=[pltpu.VMEM((SC_TOKENS,), I32), pltpu.VMEM((SC_TOKENS, width), x.dtype)],
    )
    def scatter_kernel(x_hbm, idx_hbm, out_hbm, idx_v, rows_v):
        base = (lax.axis_index("s") * n_cores + lax.axis_index("c")) * per_worker

        @pl.loop(0, per_worker // SC_TOKENS)
        def _(j):
            t0 = base + j * SC_TOKENS
            pltpu.sync_copy(x_hbm.at[pl.ds(t0, SC_TOKENS)], rows_v)
            for k in range(top_k):
                pltpu.sync_copy(idx_hbm.at[pl.ds(k * n_tok + t0, SC_TOKENS)], idx_v)
                pltpu.sync_copy(rows_v, out_hbm.at[idx_v])

    return scatter_kernel(x, dest_t.reshape(-1))


def _route(top_e):
    flat_e = top_e.reshape(-1)
    onehot = (flat_e[:, None] == jnp.arange(N_EXPERTS, dtype=I32)[None, :]).astype(I32)
    csum = jnp.cumsum(onehot, axis=0)
    counts = csum[-1]
    rank = jnp.take_along_axis(csum, flat_e[:, None], axis=1)[:, 0] - 1
    padded = (counts + MOE_BM - 1) // MOE_BM * MOE_BM
    pad_end = jnp.cumsum(padded)
    pad_start = pad_end - padded
    dest_t = (pad_start[flat_e] + rank).reshape(N_TOK, TOP_K).T
    blk_start = jnp.arange(MOE_NBLK, dtype=I32) * MOE_BM
    blk_e = jnp.minimum(jnp.sum(blk_start[:, None] >= pad_end[None, :], axis=1), N_EXPERTS - 1).astype(I32)
    n_act = (pad_end[-1] // MOE_BM).astype(I32).reshape(1)
    n_valid = jnp.clip((pad_start + counts)[blk_e] - blk_start, 0, MOE_BM).astype(I32)
    return dest_t, blk_e, n_act, n_valid


def _ln2_kernel(x_ref, y_ref, tg_ref, g_ref, b_ref, o_ref):
    tg = tg_ref[...]
    y = tg[:, 0:1] * y_ref[0]
    for r in range(1, TOP_K):
        y = y + tg[:, r:r + 1] * y_ref[r]
    o_ref[...] = _layer_norm(DEEPNORM_ALPHA * x_ref[...] + y, g_ref[...], b_ref[...])


def _ln2(x1, yk, tg, g, b):
    n = x1.shape[0]
    return pl.pallas_call(
        _ln2_kernel,
        grid=(n // TM,),
        in_specs=[pl.BlockSpec((TM, D_MODEL), lambda i: (i, 0)),
                  pl.BlockSpec((TOP_K, TM, D_MODEL), lambda i: (0, i, 0)),
                  pl.BlockSpec((TM, LANES), lambda i: (i, 0)),
                  _const_spec(g.shape), _const_spec(b.shape)],
        out_specs=pl.BlockSpec((TM, D_MODEL), lambda i: (i, 0)),
        out_shape=jax.ShapeDtypeStruct((n, D_MODEL), F32),
        compiler_params=_cparams(1),
        name="ln2",
    )(x1, yk, tg, g, b)


def _rot_cols(w, n_heads):
    lead = w.shape[:-1]
    w4 = w.reshape(*lead, n_heads, 2, HEAD_DIM // 2)
    return jnp.concatenate([-w4[..., 1, :], w4[..., 0, :]], axis=-1).reshape(*lead, n_heads * HEAD_DIM)


def _pair_major(w):
    lead = w.shape[:-1]
    w4 = w.reshape(*lead, KV_HEADS, 2, HEAD_DIM)
    return jnp.swapaxes(w4, -3, -2).reshape(*lead, ATTN_WIDTH)


def _prep_weights(w_in, w_pa, w_router, b_router):
    c = np.cumsum((ATTN_WIDTH, KV_WIDTH, KV_WIDTH, IDX_WIDTH, IDX_DIM, IDX_HEADS,
                   CONV_DIM, CONV_DIM, CONV_DIM, D_MODEL, D_MODEL)).tolist()
    wq, wk, wv = w_in[..., :c[0]], w_in[..., c[0]:c[1]], w_in[..., c[1]:c[2]]
    wqi, wki, wwi = w_in[..., c[2]:c[3]], w_in[..., c[3]:c[4]], w_in[..., c[4]:c[5]]
    wconv = w_in[..., c[5]:c[8]]
    wgate = w_in[..., c[8]:]
    wki_r = _rot_cols(wki, 1)
    wwi = jnp.pad(wwi * (IDX_HEADS ** -0.5 * IDX_DIM ** -0.5), ((0, 0), (0, 0), (0, LANES - IDX_HEADS)))
    widx = jnp.concatenate([wqi, _rot_cols(wqi, IDX_HEADS), wki, wki_r, wwi], axis=-1)
    widx_hi, w1lo = _split_hi_lo(widx)
    w1cat = jnp.concatenate(
        [_pair_major(wq).astype(BF16), _pair_major(_rot_cols(wq, N_HEADS)).astype(BF16), wk.astype(BF16),
         _rot_cols(wk, KV_HEADS).astype(BF16), wv.astype(BF16), widx_hi, wconv.astype(BF16)], axis=-1)
    wpa = jnp.swapaxes(w_pa.reshape(DEPTH, KV_HEADS, 2, HEAD_DIM, D_MODEL), 1, 2).reshape(DEPTH, ATTN_WIDTH, D_MODEL)
    wr, wrl = _split_hi_lo(jnp.pad(w_router, ((0, 0), (0, 0), (0, LANES - N_EXPERTS))))
    br = jnp.pad(b_router, ((0, 0), (0, LANES - N_EXPERTS)), constant_values=-jnp.inf)[:, None, :]
    return w1cat, w1lo, wgate.astype(BF16), wpa.astype(BF16), wr, wrl, br


def _rope_tables():
    pos = jnp.concatenate([jnp.tile(jnp.arange(SEQ, dtype=I32), BATCH),
                           jnp.tile(PAST_LEN + jnp.arange(DEC_SEQ, dtype=I32), DEC_BATCH)])
    half = HEAD_DIM // 2
    inv = ROPE_THETA ** (-jnp.arange(half, dtype=F32) / half)
    ang = pos.astype(F32)[:, None] * inv[None, :]
    return jnp.tile(jnp.cos(ang), (1, LANES // half)), jnp.tile(jnp.sin(ang), (1, LANES // half))


def kernel(x_prompt, x_sample, cache_k, cache_v, cache_kidx, state_conv, page_table, w_in, w_conv, w_pa, w_pb,
           w_o, ln1_g, ln1_b, w_router, b_router, w1, b1, w2, b2, ln2_g, ln2_b):
    w1cat, w1lo, wgate, wpa, wr, wrl, br = _prep_weights(w_in, w_pa, w_router, b_router)
    wpb = w_pb.astype(BF16)
    wo = w_o.astype(BF16)
    wc = jnp.pad(w_conv, ((0, 0), (0, 8 - w_conv.shape[1]), (0, 0)))
    b1g = b1[..., 0::2][:, :, None, :]
    b1l = b1[..., 1::2][:, :, None, :]
    b2r = b2[:, :, None, :]
    cos, sin = _rope_tables()
    zrows = jnp.zeros((DEPTH, DEC_BATCH, DEC_SEQ - 2, CONV_DIM), F32)
    h1 = jnp.concatenate([state_conv[:, :, 1:2], zrows[:, :, :1], zrows], axis=2).reshape(DEPTH, N_S, CONV_DIM)
    h2 = jnp.concatenate([state_conv, zrows], axis=2).reshape(DEPTH, N_S, CONV_DIM)
    ckT = jnp.transpose(cache_k, (0, 1, 3, 4, 2)).reshape(DEPTH, -1, KV_WIDTH, PAGE_SIZE)
    cvT = jnp.transpose(cache_v, (0, 1, 3, 4, 2)).reshape(DEPTH, -1, KV_WIDTH, PAGE_SIZE)
    ckiT = jnp.transpose(cache_kidx, (0, 1, 3, 2))
    pt = page_table.reshape(-1).astype(I32)

    x = jnp.concatenate([x_prompt.reshape(N_P, D_MODEL), x_sample.reshape(N_S, D_MODEL)], axis=0)
    ks, vs, kis, convs = [], [], [], []
    for l in range(DEPTH):
        q, qi, wi, k, v, ki4, kb, vb, kix, yc, u = _inproj(x, w1cat[l], w1lo[l], cos, sin, wc[l], h1[l], h2[l])
        a = _attn_sample(l, pt, q, qi, wi, k, v, ki4, ckT, cvT, ckiT)
        a = _attn_prompt(a, q, qi, wi, kb, vb, kix)

        x1, te, tg = _merge(x, a, yc, wgate[l], wpa[l], wpb[l], wo[l],
                            ln1_g[l][None], ln1_b[l][None], wr[l], wrl[l], br[l])
        dest_t, blk_e, n_act, n_valid = _route(te[:, :TOP_K])
        xs = _sc_scatter_rows(x1, dest_t, MOE_ROWS)
        y_rows = _moe(l, blk_e, n_act, n_valid, xs, w1, b1g, b1l, w2, b2r)
        yk = _sc_gather(y_rows, dest_t.reshape(-1)).reshape(TOP_K, N_TOK, D_MODEL)
        x = _ln2(x1, yk, tg, ln2_g[l][None], ln2_b[l][None])

        ks.append(k)
        vs.append(v)
        kis.append(ki4[:, :IDX_DIM])
        convs.append((u[:N_P].reshape(BATCH, SEQ, CONV_DIM)[:, -2:],
                      u[N_P:].reshape(DEC_BATCH, DEC_SEQ, CONV_DIM)[:, -2:]))

    def split(parts, tail):
        prompt = jnp.stack([a_[:N_P] for a_ in parts], axis=0).reshape(DEPTH, BATCH, SEQ, *tail)
        sample = jnp.stack([a_[N_P:] for a_ in parts], axis=0).reshape(DEPTH, DEC_BATCH, DEC_SEQ, *tail)
        return prompt, sample

    k_p, k_s = split(ks, (KV_HEADS, HEAD_DIM))
    v_p, v_s = split(vs, (KV_HEADS, HEAD_DIM))
    ki_p, ki_s = split(kis, (IDX_DIM,))
    conv_p = jnp.stack([c_[0] for c_ in convs], axis=0)
    conv_s = jnp.stack([c_[1] for c_ in convs], axis=0)
    y_p = x[:N_P].reshape(BATCH, SEQ, D_MODEL)
    y_s = x[N_P:].reshape(DEC_BATCH, DEC_SEQ, D_MODEL)
    return (y_p, y_s, k_p, v_p, ki_p, conv_p, k_s, v_s, ki_s, conv_s)
```
